```python
import jax, jax.numpy as jnp
from jax import lax
import numpy as np

D_MODEL = 1024
BATCH = 16
SEQ = 256
DEPTH = 4
DEC_BATCH = 2
DEC_SEQ = 2048
PAST_LEN = 512

GRID_W = 64
N_MIXERS = 2
N_GLA_LAYERS = (DEPTH + N_MIXERS - 1) // N_MIXERS
N_FNET_LAYERS = DEPTH // N_MIXERS
GLA_HEADS = 4
GLA_QK = D_MODEL // 2
GLA_V = D_MODEL
GLA_DK = GLA_QK // GLA_HEADS
GLA_DV = GLA_V // GLA_HEADS
GATE_RANK = 16
GATE_TAU = 16.0
CHUNK = 64
GLA_IN = 2 * GLA_QK + 2 * GLA_V + 2 * GATE_RANK
FNET_GROUPS = 4
FNET_GW = D_MODEL // FNET_GROUPS
FFN_DIM = 2816
CONV_W = 3
EPS = 1e-6

kernel_name = "hybrid_gla_fnet_diffusion_step"


def rmsnorm(x, g):
    xf = x.astype(jnp.float32)
    y = xf * lax.rsqrt(jnp.mean(xf * xf, axis=-1, keepdims=True) + EPS)
    return (y * g.astype(jnp.float32)).astype(x.dtype)


def ada_mod(cvec, w, b):
    m = jax.nn.silu(cvec) @ w + b
    return jnp.split(m[:, None, :], 6, axis=-1)


def gla_chunked(q, k, v, logg, s0):
    B, L, H, DK = q.shape
    DV = v.shape[-1]
    nc = L // CHUNK

    def to_chunks(t):
        return t.astype(jnp.float32).reshape(B, nc, CHUNK, H, t.shape[-1]).transpose(1, 0, 3, 2, 4)

    qc, kc, vc, gc = to_chunks(q), to_chunks(k), to_chunks(v), to_chunks(logg)
    lower = jnp.tril(jnp.ones((CHUNK, CHUNK), dtype=bool))

    def step(state, inp):
        qi, ki, vi, gi = inp
        bcum = jnp.cumsum(gi, axis=2)
        diff = bcum[:, :, :, None, :] - bcum[:, :, None, :, :]
        decay = jnp.exp(jnp.where(lower[:, :, None], diff, -jnp.inf))
        attn = jnp.einsum('bhik,bhjk,bhijk->bhij', qi, ki, decay)
        o = (jnp.einsum('bhij,bhjv->bhiv', attn, vi)
             + jnp.einsum('bhik,bhkv->bhiv', qi * jnp.exp(bcum), state))
        b_last = bcum[:, :, -1:, :]
        state = (jnp.exp(b_last[:, :, 0, :])[..., None] * state
                 + jnp.einsum('bhjk,bhjv->bhkv', ki * jnp.exp(b_last - bcum), vi))
        return state, o

    s_fin, o = lax.scan(step, s0.astype(jnp.float32), (qc, kc, vc, gc))
    o = o.transpose(1, 0, 3, 2, 4).reshape(B, L, H, DV)
    return o, s_fin


def gla_mixer(h, w_in, w_g2, b_g, norm_g, w_o, s0_f, s0_b):
    B, L, _ = h.shape
    p = h @ w_in
    q, k, v, r, glr = jnp.split(p, [GLA_QK, 2 * GLA_QK, 2 * GLA_QK + GLA_V, 2 * GLA_QK + 2 * GLA_V], axis=-1)
    q = q.reshape(B, L, GLA_HEADS, GLA_DK) * (GLA_DK ** -0.5)
    k = k.reshape(B, L, GLA_HEADS, GLA_DK)
    v = v.reshape(B, L, GLA_HEADS, GLA_DV)
    glr = glr.reshape(B, L, 2, GATE_RANK)
    logit = jnp.einsum('blzr,zrk->blzk', glr, w_g2) + b_g
    logg = (jax.nn.log_sigmoid(logit.astype(jnp.float32)) / GATE_TAU).reshape(B, L, 2, GLA_HEADS, GLA_DK)
    o_f, s_f = gla_chunked(q, k, v, logg[:, :, 0], s0_f)
    o_b, s_b = gla_chunked(jnp.flip(q, 1), jnp.flip(k, 1), jnp.flip(v, 1), jnp.flip(logg[:, :, 1], 1), s0_b)
    o = (o_f + jnp.flip(o_b, 1)).astype(h.dtype)
    o = rmsnorm(o, norm_g) * jax.nn.silu(r.reshape(B, L, GLA_HEADS, GLA_DV))
    return o.reshape(B, L, GLA_V) @ w_o, s_f, s_b


def fourier_mixer(h, w_in, w_o):
    B, L, _ = h.shape
    u = (h @ w_in).reshape(B, L, FNET_GROUPS, FNET_GW).astype(jnp.float32)
    f = jnp.fft.fftn(u, axes=(1, 3), norm='ortho').real
    return f.astype(h.dtype).reshape(B, L, D_MODEL) @ w_o


def dwconv3(a, w, b):
    ap = jnp.pad(a, [(0, 0)] * (a.ndim - 2) + [(1, 1), (0, 0)])
    return ap[..., :-2, :] * w[0] + ap[..., 1:-1, :] * w[1] + ap[..., 2:, :] * w[2] + b


def conv_ffn(h, w_up, conv_w, conv_b, w_down, rows):
    B, L, _ = h.shape
    a, g = jnp.split(h @ w_up, 2, axis=-1)
    if rows is None:
        a = dwconv3(a, conv_w, conv_b)
    else:
        a = dwconv3(a.reshape(B, rows, GRID_W, FFN_DIM), conv_w, conv_b).reshape(B, L, FFN_DIM)
    return (jax.nn.silu(a) * g) @ w_down


def setup_inputs(seed: int = 0) -> dict:
    key = jax.random.key(seed)
    ks = jax.random.split(key, 24)
    f32 = jnp.float32
    nrm = lambda k, shape, s: jax.random.normal(k, shape, f32) * s
    D = D_MODEL
    return {
        'x_prompt': nrm(ks[0], (BATCH, SEQ, D), 1.0),
        'x_sample': nrm(ks[1], (DEC_BATCH, DEC_SEQ, D), 1.0),
        'state_gla': nrm(ks[2], (DEC_BATCH, N_GLA_LAYERS, 2, GLA_HEADS, GLA_DK, GLA_DV), 0.5),
        'c': nrm(ks[3], (DEC_BATCH, D), 1.0),
        'c_ctx': nrm(ks[4], (D,), 1.0),
        'norm_mix_g': 1.0 + nrm(ks[5], (DEPTH, D), 0.02),
        'norm_ffn_g': 1.0 + nrm(ks[6], (DEPTH, D), 0.02),
        'w_mod': nrm(ks[7], (DEPTH, D, 6 * D), 0.5 * D ** -0.5),
        'b_mod': nrm(ks[8], (DEPTH, 6 * D), 0.02),
        'gla_w_in': nrm(ks[9], (N_GLA_LAYERS, D, GLA_IN), D ** -0.5),
        'gla_w_g2': nrm(ks[10], (N_GLA_LAYERS, 2, GATE_RANK, GLA_QK), GATE_RANK ** -0.5),
        'gla_b_g': nrm(ks[11], (N_GLA_LAYERS, 2, GLA_QK), 0.1),
        'gla_norm_g': 1.0 + nrm(ks[12], (N_GLA_LAYERS, GLA_DV), 0.02),
        'gla_w_o': nrm(ks[13], (N_GLA_LAYERS, GLA_V, D), GLA_V ** -0.5),
        'fnet_w_in': nrm(ks[14], (N_FNET_LAYERS, D, D), D ** -0.5),
        'fnet_w_o': nrm(ks[15], (N_FNET_LAYERS, D, D), D ** -0.5),
        'ffn_w_up': nrm(ks[16], (DEPTH, D, 2 * FFN_DIM), D ** -0.5),
        'ffn_conv_w': nrm(ks[17], (DEPTH, CONV_W, FFN_DIM), CONV_W ** -0.5),
        'ffn_conv_b': nrm(ks[18], (DEPTH, FFN_DIM), 0.02),
        'ffn_w_down': nrm(ks[19], (DEPTH, FFN_DIM, D), FFN_DIM ** -0.5),
        'final_norm_g': 1.0 + nrm(ks[20], (D,), 0.02),
    }


def reference(x_prompt, x_sample, state_gla, c, c_ctx, norm_mix_g, norm_ffn_g, w_mod, b_mod,
              gla_w_in, gla_w_g2, gla_b_g, gla_norm_g, gla_w_o, fnet_w_in, fnet_w_o,
              ffn_w_up, ffn_conv_w, ffn_conv_b, ffn_w_down, final_norm_g):
    ctx = x_prompt
    lat = x_sample
    bsz = ctx.shape[0]
    rows = lat.shape[1] // GRID_W
    zero_state = jnp.zeros((bsz, GLA_HEADS, GLA_DK, GLA_DV), jnp.float32)
    new_states = []
    for i in range(DEPTH):
        sh1c, sc1c, gt1c, sh2c, sc2c, gt2c = ada_mod(c_ctx[None, :], w_mod[i], b_mod[i])
        sh1l, sc1l, gt1l, sh2l, sc2l, gt2l = ada_mod(c, w_mod[i], b_mod[i])
        hc = rmsnorm(ctx, norm_mix_g[i]) * (1 + sc1c) + sh1c
        hl = rmsnorm(lat, norm_mix_g[i]) * (1 + sc1l) + sh1l
        if i % N_MIXERS == 0:
            j = i // N_MIXERS
            yc, s_f, s_b = gla_mixer(hc, gla_w_in[j], gla_w_g2[j], gla_b_g[j], gla_norm_g[j], gla_w_o[j],
                                     zero_state, zero_state)
            new_states.append(jnp.stack([s_f, s_b], axis=1).astype(ctx.dtype))
            yl, _, _ = gla_mixer(hl, gla_w_in[j], gla_w_g2[j], gla_b_g[j], gla_norm_g[j], gla_w_o[j],
                                 state_gla[:, j, 0], state_gla[:, j, 1])
        else:
            j = i // N_MIXERS
            yc = fourier_mixer(hc, fnet_w_in[j], fnet_w_o[j])
            yl = fourier_mixer(hl, fnet_w_in[j], fnet_w_o[j])
        ctx = ctx + gt1c * yc
        lat = lat + gt1l * yl
        hc = rmsnorm(ctx, norm_ffn_g[i]) * (1 + sc2c) + sh2c
        hl = rmsnorm(lat, norm_ffn_g[i]) * (1 + sc2l) + sh2l
        ctx = ctx + gt2c * conv_ffn(hc, ffn_w_up[i], ffn_conv_w[i], ffn_conv_b[i], ffn_w_down[i], None)
        lat = lat + gt2l * conv_ffn(hl, ffn_w_up[i], ffn_conv_w[i], ffn_conv_b[i], ffn_w_down[i], rows)
    y_prompt = rmsnorm(ctx, final_norm_g)
    y_sample = rmsnorm(lat, final_norm_g)
    new_state_gla = jnp.stack(new_states, axis=1)
    return (y_prompt, y_sample, new_state_gla)
```

```python
import functools

import numpy as np
import jax
import jax.numpy as jnp
from jax import lax
from jax.experimental import pallas as pl
from jax.experimental.pallas import tpu as pltpu

F32 = jnp.float32
BF16 = jnp.bfloat16

D_MODEL = 1024
BATCH = 16
SEQ = 256
DEPTH = 4
DEC_BATCH = 2
DEC_SEQ = 2048
GRID_W = 64
GLA_HEADS = 4
GLA_QK = D_MODEL // 2
GLA_V = D_MODEL
GLA_DK = GLA_QK // GLA_HEADS
GLA_DV = GLA_V // GLA_HEADS
GATE_RANK = 16
GATE_TAU = 16.0
GLA_IN = 2 * GLA_QK + 2 * GLA_V + 2 * GATE_RANK
FNET_GROUPS = 4
FNET_GW = D_MODEL // FNET_GROUPS
FFN_DIM = 2816
EPS = 1e-6

N_CTX_TOK = BATCH * SEQ
N_LAT_TOK = DEC_BATCH * DEC_SEQ
N_TOK = N_CTX_TOK + N_LAT_TOK
N_GROUPS = 1 + DEC_BATCH

CHUNK = 64
SUPER = 256
TM = 512
SCAN_ROWS = 2048
TF = 1408
GLA_IN_PAD = 3328
GLR_COL = 2 * GLA_QK + 2 * GLA_V
VMEM_LIMIT = 56 * 1024 * 1024
MOD_TN = 1024


def _dot(a, b):
    return jnp.dot(a, b, preferred_element_type=F32)


def _dot_nt(a, b):
    return lax.dot_general(a, b, (((1,), (1,)), ((), ())), preferred_element_type=F32)


def _dot_tn(a, b):
    return lax.dot_general(a, b, (((0,), (0,)), ((), ())), preferred_element_type=F32)


def _silu(x):
    return x / (1.0 + jnp.exp(-x))


def _log_sigmoid(x):
    return jnp.minimum(x, 0.0) - jnp.log(1.0 + jnp.exp(-jnp.abs(x)))


def _rms(x, g):
    return x * lax.rsqrt(jnp.mean(x * x, axis=-1, keepdims=True) + EPS) * g


def _group_of_block(i, rows):
    n_ctx = N_CTX_TOK // rows
    per_lat = DEC_SEQ // rows
    return jnp.where(i < n_ctx, 0, 1 + (i - n_ctx) // per_lat)


def _params(sem):
    return pltpu.CompilerParams(dimension_semantics=sem, vmem_limit_bytes=VMEM_LIMIT)


def _const_spec(shape):
    nd = len(shape)
    return pl.BlockSpec(shape, lambda *_: (0,) * nd, pipeline_mode=pl.Buffered(1))


def _mods_kernel(c_ref, w_ref, b_ref, o_ref):
    w = w_ref[...]
    for v in range(N_GROUPS):
        s = _silu(c_ref[v])
        o_ref[v] = jnp.sum(w * s, axis=0, keepdims=True) + b_ref[...]


def _mods(c_cols, w_mod, b_mod):
    n_out = 6 * D_MODEL
    return pl.pallas_call(
        _mods_kernel,
        grid=(DEPTH, n_out // MOD_TN),
        in_specs=[
            pl.BlockSpec((N_GROUPS, D_MODEL, 1), lambda l, n: (0, 0, 0)),
            pl.BlockSpec((None, D_MODEL, MOD_TN), lambda l, n: (l, 0, n)),
            pl.BlockSpec((None, 1, MOD_TN), lambda l, n: (l, 0, n)),
        ],
        out_specs=pl.BlockSpec((None, N_GROUPS, 1, MOD_TN), lambda l, n: (l, 0, 0, n)),
        out_shape=jax.ShapeDtypeStruct((DEPTH, N_GROUPS, 1, n_out), F32),
        compiler_params=_params(("arbitrary", "arbitrary")),
        name="adaln_mods",
    )(c_cols, w_mod, b_mod.reshape(DEPTH, 1, n_out))


def _mod_spec(layer):
    return pl.BlockSpec((None, None, 1, 6 * D_MODEL),
                        lambda i, *_: (layer, _group_of_block(i, TM), 0, 0))


def _mod_part(mod_ref, k):
    return mod_ref[:, k * D_MODEL:(k + 1) * D_MODEL]


def _cumsum3(t, g):
    g1 = g.astype(BF16)
    r1 = g - g1.astype(F32)
    g2 = r1.astype(BF16)
    g3 = (r1 - g2.astype(F32)).astype(BF16)
    return _dot(t, g1) + _dot(t, g2) + _dot(t, g3)


def _rows_bcast(b, first_row):
    n = b.shape[0] // CHUNK
    return jnp.concatenate(
        [jnp.broadcast_to(b[c * CHUNK + first_row:c * CHUNK + first_row + 1, :], (CHUNK, b.shape[1]))
         for c in range(n)], axis=0)


def _gla_in_kernel(x_ref, mod_ref, ng_ref, w_ref, wg2_ref, bg_ref, tlow_ref, tup_ref,
                   qf_ref, kf_ref, qb_ref, kb_ref, v_ref, r_ref, oi_ref, df_ref, db_ref):
    h = _rms(x_ref[...], ng_ref[...]) * (1.0 + _mod_part(mod_ref, 1)) + _mod_part(mod_ref, 0)
    hb = h.astype(BF16)
    qk = _dot(hb, w_ref[:, 0:2 * GLA_QK])
    q = qk[:, 0:GLA_QK] * (GLA_DK ** -0.5)
    k = qk[:, GLA_QK:2 * GLA_QK]
    vb = _dot(hb, w_ref[:, 2 * GLA_QK:2 * GLA_QK + GLA_V]).astype(BF16)
    v_ref[...] = vb
    r_ref[...] = _dot(hb, w_ref[:, 2 * GLA_QK + GLA_V:GLR_COL])
    glr = _dot(hb, w_ref[:, GLR_COL:GLA_IN_PAD]).astype(BF16)
    inv_tau = 1.0 / GATE_TAU
    gf = _log_sigmoid(_dot(glr, wg2_ref[0]) + bg_ref[0:1, :]) * inv_tau
    gb = _log_sigmoid(_dot(glr, wg2_ref[1]) + bg_ref[1:2, :]) * inv_tau

    ri = lax.broadcasted_iota(jnp.int32, (SUPER, SUPER), 0)
    ci = lax.broadcasted_iota(jnp.int32, (SUPER, SUPER), 1)
    same = (ri & -CHUNK) == (ci & -CHUNK)
    low = same & (ci <= ri)
    up = same & (ci >= ri)
    cpg = SUPER // CHUNK

    for s in range(TM // SUPER):
        rows = slice(s * SUPER, (s + 1) * SUPER)
        qs, ks = q[rows], k[rows]
        b = _cumsum3(tlow_ref[...], gf[rows])
        bmid = _rows_bcast(b, CHUNK // 2 - 1)
        blast = _rows_bcast(b, CHUNK - 1)
        qtf = (qs * jnp.exp(b - bmid)).astype(BF16)
        ktf = (ks * jnp.exp(bmid - b)).astype(BF16)
        qf_ref[rows, :] = (qs * jnp.exp(b)).astype(BF16)
        kf_ref[rows, :] = (ks * jnp.exp(blast - b)).astype(BF16)
        for c in range(cpg):
            df_ref[s * cpg + c:s * cpg + c + 1, :] = jnp.exp(b[c * CHUNK + CHUNK - 1:(c + 1) * CHUNK, :])
        b = _cumsum3(tup_ref[...], gb[rows])
        bmid = _rows_bcast(b, CHUNK // 2)
        blast = _rows_bcast(b, 0)
        qtb = (qs * jnp.exp(b - bmid)).astype(BF16)
        ktb = (ks * jnp.exp(bmid - b)).astype(BF16)
        qb_ref[rows, :] = (qs * jnp.exp(b)).astype(BF16)
        kb_ref[rows, :] = (ks * jnp.exp(blast - b)).astype(BF16)
        for c in range(cpg):
            db_ref[s * cpg + c:s * cpg + c + 1, :] = jnp.exp(b[c * CHUNK:c * CHUNK + 1, :])
        for hd in range(GLA_HEADS):
            kc = slice(hd * GLA_DK, (hd + 1) * GLA_DK)
            af = _dot_nt(qtf[:, kc], ktf[:, kc])
            ab = _dot_nt(qtb[:, kc], ktb[:, kc])
            a = (jnp.where(low, af, 0.0) + jnp.where(up, ab, 0.0)).astype(BF16)
            vc = slice(hd * GLA_DV, (hd + 1) * GLA_DV)
            oi_ref[rows, vc] = _dot(a, vb[rows, vc])


def _gla_in(x, mods, layer, norm_g, w_in_pad, wg2_pad, b_g, tlow, tup):
    nblk = N_TOK // TM
    cpb = TM // CHUNK
    row = lambda w: pl.BlockSpec((TM, w), lambda i: (i, 0))
    outs = [
        jax.ShapeDtypeStruct((N_TOK, GLA_QK), BF16),
        jax.ShapeDtypeStruct((N_TOK, GLA_QK), BF16),
        jax.ShapeDtypeStruct((N_TOK, GLA_QK), BF16),
        jax.ShapeDtypeStruct((N_TOK, GLA_QK), BF16),
        jax.ShapeDtypeStruct((N_TOK, GLA_V), BF16),
        jax.ShapeDtypeStruct((N_TOK, GLA_V), F32),
        jax.ShapeDtypeStruct((N_TOK, GLA_V), F32),
        jax.ShapeDtypeStruct((N_TOK // CHUNK, GLA_QK), F32),
        jax.ShapeDtypeStruct((N_TOK // CHUNK, GLA_QK), F32),
    ]
    return pl.pallas_call(
        _gla_in_kernel,
        grid=(nblk,),
        in_specs=[
            row(D_MODEL), _mod_spec(layer), _const_spec((1, D_MODEL)),
            _const_spec((D_MODEL, GLA_IN_PAD)), _const_spec((2, 256, GLA_QK)),
            _const_spec((2, GLA_QK)), _const_spec((SUPER, SUPER)), _const_spec((SUPER, SUPER)),
        ],
        out_specs=[row(GLA_QK)] * 4 + [row(GLA_V)] * 3
        + [pl.BlockSpec((cpb, GLA_QK), lambda i: (i, 0))] * 2,
        out_shape=outs,
        compiler_params=_params(("arbitrary",)),
        name="gla_in",
    )(x, mods, norm_g, w_in_pad, wg2_pad, b_g, tlow, tup)


SCAN_UNROLL = 4


def _gla_scan_kernel(qf_ref, kf_ref, qb_ref, kb_ref, v_ref, oi_ref, df_ref, db_ref, s0_ref,
                     o_ref, st_ref, s_scr):
    i = pl.program_id(1)
    o_ref[...] = oi_ref[...]
    q_refs, k_refs, d_refs = (qf_ref, qb_ref), (kf_ref, kb_ref), (df_ref, db_ref)

    def chunk_step(direction, chunk):
        r0 = pl.multiple_of(chunk * CHUNK, CHUNK)
        q = q_refs[direction][pl.ds(r0, CHUNK), :]
        k = k_refs[direction][pl.ds(r0, CHUNK), :]
        v = v_ref[pl.ds(r0, CHUNK), :]
        d = d_refs[direction][pl.ds(chunk, 1), :]
        st = s_scr[direction]
        o_ref[pl.ds(r0, CHUNK), :] += _dot_nt(q, st.astype(BF16))
        s_scr[direction] = st * d + _dot_tn(v, k)

    @pl.when(i < N_CTX_TOK // SCAN_ROWS)
    def _():
        cps = SEQ // CHUNK

        def seq_body(s, carry):
            s_scr[...] = jnp.zeros_like(s_scr)
            for u in range(cps):
                chunk_step(0, s * cps + u)
                chunk_step(1, s * cps + (cps - 1 - u))
            st_ref[s, 0] = s_scr[0].T
            st_ref[s, 1] = s_scr[1].T
            return carry

        lax.fori_loop(0, SCAN_ROWS // SEQ, seq_body, 0)

    @pl.when(i >= N_CTX_TOK // SCAN_ROWS)
    def _():
        n_chunks = SCAN_ROWS // CHUNK
        s_scr[...] = s0_ref[...]

        def body(it, carry):
            for u in range(SCAN_UNROLL):
                cf = it * SCAN_UNROLL + u
                chunk_step(0, cf)
                chunk_step(1, n_chunks - 1 - cf)
            return carry

        lax.fori_loop(0, n_chunks // SCAN_UNROLL, body, 0)


def _gla_scan(qf, kf, qb, kb, v, oi, df, db, s0t):
    nblk = N_TOK // SCAN_ROWS
    n_ctx_blk = N_CTX_TOK // SCAN_ROWS
    cpb = SCAN_ROWS // CHUNK
    qk_spec = pl.BlockSpec((SCAN_ROWS, GLA_DK), lambda h, i: (i, h))
    v_spec = pl.BlockSpec((SCAN_ROWS, GLA_DV), lambda h, i: (i, h))
    d_spec = pl.BlockSpec((cpb, GLA_DK), lambda h, i: (i, h))
    seq_per_blk = SCAN_ROWS // SEQ
    return pl.pallas_call(
        _gla_scan_kernel,
        grid=(GLA_HEADS, nblk),
        in_specs=[qk_spec, qk_spec, qk_spec, qk_spec, v_spec, v_spec, d_spec, d_spec,
                  pl.BlockSpec((None, 2, None, GLA_DV, GLA_DK), lambda h, i: (i, 0, h, 0, 0))],
        out_specs=[
            v_spec,
            pl.BlockSpec((seq_per_blk, 2, None, GLA_DK, GLA_DV),
                         lambda h, i: (jnp.minimum(i, n_ctx_blk - 1), 0, h, 0, 0)),
        ],
        out_shape=[jax.ShapeDtypeStruct((N_TOK, GLA_V), F32),
                   jax.ShapeDtypeStruct((BATCH, 2, GLA_HEADS, GLA_DK, GLA_DV), F32)],
        scratch_shapes=[pltpu.VMEM((2, GLA_DV, GLA_DK), F32)],
        compiler_params=_params(("arbitrary", "arbitrary")),
        name="gla_scan",
    )(qf, kf, qb, kb, v, oi, df, db, s0t)


def _fnet_in_kernel(x_ref, mod_ref, ng_ref, w_ref, cs_ref, uc_ref, us_ref):
    h = _rms(x_ref[...], ng_ref[...]) * (1.0 + _mod_part(mod_ref, 1)) + _mod_part(mod_ref, 0)
    u = _dot(h.astype(BF16), w_ref[...]).astype(BF16)
    for g in range(FNET_GROUPS):
        cols = slice(g * FNET_GW, (g + 1) * FNET_GW)
        t = _dot(u[:, cols], cs_ref[...])
        uc_ref[:, cols] = t[:, 0:FNET_GW].astype(BF16)
        us_ref[:, cols] = t[:, FNET_GW:2 * FNET_GW].astype(BF16)


def _fnet_in(x, mods, layer, norm_g, w_in, cs_chan):
    row = pl.BlockSpec((TM, D_MODEL), lambda i: (i, 0))
    return pl.pallas_call(
        _fnet_in_kernel,
        grid=(N_TOK // TM,),
        in_specs=[row, _mod_spec(layer), _const_spec((1, D_MODEL)),
                  _const_spec((D_MODEL, D_MODEL)), _const_spec((FNET_GW, 2 * FNET_GW))],
        out_specs=[row, row],
        out_shape=[jax.ShapeDtypeStruct((N_TOK, D_MODEL), BF16)] * 2,
        compiler_params=_params(("arbitrary",)),
        name="fnet_in",
    )(x, mods, norm_g, w_in, cs_chan)


def _dft_ctx_kernel(uc_ref, us_ref, c_ref, s_ref, f_ref):
    scale = SEQ ** -0.5
    for s in range(SCAN_ROWS // SEQ):
        rows = slice(s * SEQ, (s + 1) * SEQ)
        f = _dot(c_ref[...], uc_ref[rows, :]) - _dot(s_ref[...], us_ref[rows, :])
        f_ref[rows, :] = (f * scale).astype(BF16)


def _dft_lat_kernel(uc_ref, us_ref, c_ref, s_ref, f_ref):
    f = _dot(c_ref[...], uc_ref[...]) - _dot(s_ref[...], us_ref[...])
    f_ref[...] = (f * DEC_SEQ ** -0.5).astype(BF16)


def _fnet_dft(uc, us, c_ctx, s_ctx, c_lat, s_lat):
    blk = pl.BlockSpec((SCAN_ROWS, D_MODEL), lambda i: (i, 0))
    f_ctx = pl.pallas_call(
        _dft_ctx_kernel,
        grid=(N_CTX_TOK // SCAN_ROWS,),
        in_specs=[blk, blk, _const_spec((SEQ, SEQ)), _const_spec((SEQ, SEQ))],
        out_specs=blk,
        out_shape=jax.ShapeDtypeStruct((N_CTX_TOK, D_MODEL), BF16),
        compiler_params=_params(("arbitrary",)),
        name="fnet_dft_ctx",
    )(uc, us, c_ctx, s_ctx)
    n_ctx_blk = N_CTX_TOK // DEC_SEQ
    m_tiles = DEC_SEQ // TM
    seq_blk = pl.BlockSpec((DEC_SEQ, D_MODEL), lambda b, m: (n_ctx_blk + b, 0))
    tab = pl.BlockSpec((TM, DEC_SEQ), lambda b, m: (m, 0))
    f_lat = pl.pallas_call(
        _dft_lat_kernel,
        grid=(DEC_BATCH, m_tiles),
        in_specs=[seq_blk, seq_blk, tab, tab],
        out_specs=pl.BlockSpec((TM, D_MODEL), lambda b, m: (b * m_tiles + m, 0)),
        out_shape=jax.ShapeDtypeStruct((N_LAT_TOK, D_MODEL), BF16),
        compiler_params=_params(("arbitrary", "arbitrary")),
        name="fnet_dft_lat",
    )(uc, us, c_lat, s_lat)
    return jnp.concatenate([f_ctx, f_lat], axis=0)


def _mix_ffn_kernel(is_gla, is_last, *refs):
    if is_gla:
        x_ref, o_ref, r_ref, gn_ref = refs[:4]
        refs = refs[4:]
    else:
        x_ref, f_ref = refs[:2]
        refs = refs[2:]
    (wo_ref, mod_ref, ng_ref, wa_ref, wg_ref, cw_ref, cb_ref, wd_ref, fg_ref,
     out_ref, x_scr, h_scr, acc_scr) = refs
    i = pl.program_id(0)
    j = pl.program_id(1)

    @pl.when(j == 0)
    def _():
        if is_gla:
            parts = []
            for hd in range(GLA_HEADS):
                cols = slice(hd * GLA_DV, (hd + 1) * GLA_DV)
                y = _rms(o_ref[:, cols], gn_ref[...]) * _silu(r_ref[:, cols])
                parts.append(y.astype(BF16))
            y = jnp.concatenate(parts, axis=1)
        else:
            y = f_ref[...]
        x_new = x_ref[...] + _mod_part(mod_ref, 2) * _dot(y, wo_ref[...])
        x_scr[...] = x_new
        h = _rms(x_new, ng_ref[...]) * (1.0 + _mod_part(mod_ref, 4)) + _mod_part(mod_ref, 3)
        h_scr[...] = h.astype(BF16)
        acc_scr[...] = jnp.zeros_like(acc_scr)

    hb = h_scr[...]
    a = _dot(hb, wa_ref[...])
    g = _dot(hb, wg_ref[...])
    period = jnp.where(i < N_CTX_TOK // TM, SEQ, GRID_W)
    pos = lax.broadcasted_iota(jnp.int32, (TM, 1), 0) & (period - 1)
    has_prev = (pos != 0).astype(F32)
    has_next = (pos != period - 1).astype(F32)
    a_prev = pltpu.roll(a, 1, 0) * has_prev
    a_next = pltpu.roll(a, TM - 1, 0) * has_next
    conv = a_prev * cw_ref[0:1, :] + a * cw_ref[1:2, :] + a_next * cw_ref[2:3, :] + cb_ref[...]
    acc_scr[...] += _dot((_silu(conv) * g).astype(BF16), wd_ref[...])

    @pl.when(j == pl.num_programs(1) - 1)
    def _():
        out = x_scr[...] + _mod_part(mod_ref, 5) * acc_scr[...]
        if is_last:
            out = _rms(out, fg_ref[...])
        out_ref[...] = out


def _mix_ffn(is_gla, is_last, x, mix_in, w_o, mods, layer, norm_g, w_up, conv_w, conv_b, w_down,
             final_g):
    n_f = FFN_DIM // TF
    row = pl.BlockSpec((TM, D_MODEL), lambda i, j: (i, 0))
    n_mix = len(mix_in)
    mix_specs = [row, row] + ([_const_spec((1, GLA_DV))] if is_gla else [])
    in_specs = [row] + mix_specs[:n_mix] + [
        _const_spec((D_MODEL, D_MODEL)), _mod_spec(layer), _const_spec((1, D_MODEL)),
        pl.BlockSpec((D_MODEL, TF), lambda i, j: (0, j)),
        pl.BlockSpec((D_MODEL, TF), lambda i, j: (0, n_f + j)),
        pl.BlockSpec((3, TF), lambda i, j: (0, j)),
        pl.BlockSpec((1, TF), lambda i, j: (0, j)),
        pl.BlockSpec((TF, D_MODEL), lambda i, j: (j, 0)),
        _const_spec((1, D_MODEL)),
    ]
    return pl.pallas_call(
        functools.partial(_mix_ffn_kernel, is_gla, is_last),
        grid=(N_TOK // TM, n_f),
        in_specs=in_specs,
        out_specs=row,
        out_shape=jax.ShapeDtypeStruct((N_TOK, D_MODEL), F32),
        scratch_shapes=[pltpu.VMEM((TM, D_MODEL), F32), pltpu.VMEM((TM, D_MODEL), BF16),
                        pltpu.VMEM((TM, D_MODEL), F32)],
        compiler_params=_params(("arbitrary", "arbitrary")),
        name="gla_out_ffn" if is_gla else "fnet_out_ffn",
    )(x, *mix_in, w_o, mods, norm_g, w_up, w_up, conv_w, conv_b, w_down, final_g)


def _dft_tables(n, scale):
    idx = np.arange(n, dtype=np.int64)
    ang = (2.0 * np.pi / n) * ((idx[:, None] * idx[None, :]) % n).astype(np.float64)
    return (np.cos(ang) * scale).astype(np.float32), (np.sin(ang) * scale).astype(np.float32)


def _chunk_triangles():
    idx = np.arange(SUPER)
    same = (idx[:, None] // CHUNK) == (idx[None, :] // CHUNK)
    low = same & (idx[None, :] <= idx[:, None])
    up = same & (idx[None, :] >= idx[:, None])
    return low.astype(np.float32), up.astype(np.float32)


def kernel(x_prompt, x_sample, state_gla, c, c_ctx, norm_mix_g, norm_ffn_g, w_mod, b_mod,
           gla_w_in, gla_w_g2, gla_b_g, gla_norm_g, gla_w_o, fnet_w_in, fnet_w_o,
           ffn_w_up, ffn_conv_w, ffn_conv_b, ffn_w_down, final_norm_g):
    x = jnp.concatenate([x_prompt.reshape(N_CTX_TOK, D_MODEL),
                         x_sample.reshape(N_LAT_TOK, D_MODEL)], axis=0)
    c_cols = jnp.concatenate([c_ctx[None, :], c], axis=0)[:, :, None]
    mods = _mods(c_cols, w_mod, b_mod)

    tlow_np, tup_np = _chunk_triangles()
    tlow, tup = jnp.asarray(tlow_np).astype(BF16), jnp.asarray(tup_np).astype(BF16)
    cc, sc = _dft_tables(FNET_GW, FNET_GW ** -0.5)
    cs_chan = jnp.asarray(np.concatenate([cc, sc], axis=1)).astype(BF16)
    c_ctx_tab, s_ctx_tab = (jnp.asarray(t).astype(BF16) for t in _dft_tables(SEQ, 1.0))
    c_lat_tab, s_lat_tab = (jnp.asarray(t).astype(BF16) for t in _dft_tables(DEC_SEQ, 1.0))
    final_g = final_norm_g.reshape(1, D_MODEL)

    new_states = []
    for layer in range(DEPTH):
        j = layer // 2
        ng_mix = norm_mix_g[layer].reshape(1, D_MODEL)
        if layer % 2 == 0:
            w_in_pad = jnp.pad(gla_w_in[j], ((0, 0), (0, GLA_IN_PAD - GLA_IN))).astype(BF16)
            wg2_pad = jnp.zeros((2, 256, GLA_QK), F32)
            wg2_pad = wg2_pad.at[0, 0:GATE_RANK].set(gla_w_g2[j, 0])
            wg2_pad = wg2_pad.at[1, GATE_RANK:2 * GATE_RANK].set(gla_w_g2[j, 1]).astype(BF16)
            qf, kf, qb, kb, v, r, oi, df, db = _gla_in(
                x, mods, layer, ng_mix, w_in_pad, wg2_pad, gla_b_g[j], tlow, tup)
            s0t = jnp.concatenate([
                jnp.zeros((N_CTX_TOK // SCAN_ROWS, 2, GLA_HEADS, GLA_DV, GLA_DK), F32),
                jnp.swapaxes(state_gla[:, j], -1, -2)], axis=0)
            o, st = _gla_scan(qf, kf, qb, kb, v, oi, df, db, s0t)
            new_states.append(st)
            mix_in = (o, r, gla_norm_g[j].reshape(1, GLA_DV))
            w_o = gla_w_o[j]
        else:
            uc, us = _fnet_in(x, mods, layer, ng_mix, fnet_w_in[j].astype(BF16), cs_chan)
            f = _fnet_dft(uc, us, c_ctx_tab, s_ctx_tab, c_lat_tab, s_lat_tab)
            mix_in = (f,)
            w_o = fnet_w_o[j]
        x = _mix_ffn(layer % 2 == 0, layer == DEPTH - 1, x, mix_in, w_o.astype(BF16), mods, layer,
                     norm_ffn_g[layer].reshape(1, D_MODEL), ffn_w_up[layer].astype(BF16),
                     ffn_conv_w[layer], ffn_conv_b[layer].reshape(1, FFN_DIM),
                     ffn_w_down[layer].astype(BF16), final_g)

    y_prompt = x[:N_CTX_TOK].reshape(BATCH, SEQ, D_MODEL)
    y_sample = x[N_CTX_TOK:].reshape(DEC_BATCH, DEC_SEQ, D_MODEL)
    return y_prompt, y_sample, jnp.stack(new_states, axis=1)
```

```python
import functools

import numpy as np
import jax
import jax.numpy as jnp
from jax import lax
from jax.experimental import pallas as pl
from jax.experimental.pallas import tpu as pltpu

F32 = jnp.float32
BF16 = jnp.bfloat16

D_MODEL = 1024
BATCH = 16
SEQ = 256
DEPTH = 4
DEC_BATCH = 2
DEC_SEQ = 2048
GRID_W = 64
N_GLA_LAYERS = 2
GLA_HEADS = 4
GLA_QK = D_MODEL // 2
GLA_V = D_MODEL
GLA_DK = GLA_QK // GLA_HEADS
GLA_DV = GLA_V // GLA_HEADS
GATE_RANK = 16
GATE_TAU = 16.0
GLA_IN = 2 * GLA_QK + 2 * GLA_V + 2 * GATE_RANK
FNET_GROUPS = 4
FNET_GW = D_MODEL // FNET_GROUPS
FFN_DIM = 2816
EPS = 1e-6

N_CTX_TOK = BATCH * SEQ
N_LAT_TOK = DEC_BATCH * DEC_SEQ
N_TOK = N_CTX_TOK + N_LAT_TOK
N_GROUPS = 1 + DEC_BATCH

LANES = 128
SUBLANES = 8
CHUNK = 64
SUPER = 256
TM = 512
N_CTX_BLK = N_CTX_TOK // TM
SCAN_ROWS = 2048
N_CTX_SCAN = N_CTX_TOK // SCAN_ROWS
SCAN_STEPS = 8
HALF = 256
TF = 1408
GLA_IN_PAD = 3328
GLR_COL = 2 * GLA_QK + 2 * GLA_V
VMEM_LIMIT = 56 * 1024 * 1024
MOD_TN = 1024


def _dot(a, b):
    return jnp.dot(a, b, preferred_element_type=F32)


def _dot_nt(a, b):
    return lax.dot_general(a, b, (((1,), (1,)), ((), ())), preferred_element_type=F32)


def _dot_tn(a, b):
    return lax.dot_general(a, b, (((0,), (0,)), ((), ())), preferred_element_type=F32)


def _silu(x):
    return x / (1.0 + jnp.exp(-x))


def _log_sigmoid(x):
    return jnp.minimum(x, 0.0) - jnp.log(1.0 + jnp.exp(-jnp.abs(x)))


def _rms(x, g):
    return x * lax.rsqrt(jnp.mean(x * x, axis=-1, keepdims=True) + EPS) * g


def _group_of_block(i):
    return jnp.where(i < N_CTX_BLK, 0, 1 + (i - N_CTX_BLK) // (DEC_SEQ // TM))


def _params(sem):
    return pltpu.CompilerParams(dimension_semantics=sem, vmem_limit_bytes=VMEM_LIMIT)


def _const_spec(shape):
    nd = len(shape)
    return pl.BlockSpec(shape, lambda *_: (0,) * nd, pipeline_mode=pl.Buffered(1))


def _mod_spec(layer):
    return pl.BlockSpec((None, None, 1, 6 * D_MODEL),
                        lambda i, *_: (layer, _group_of_block(i), 0, 0))


def _mod_part(mod_ref, k):
    return mod_ref[:, k * D_MODEL:(k + 1) * D_MODEL]


def _ctx_rows(width=D_MODEL):
    return pl.BlockSpec((TM, width), lambda i, *_: (jnp.minimum(i, N_CTX_BLK - 1), 0))


def _lat_rows(width=D_MODEL):
    return pl.BlockSpec((TM, width), lambda i, *_: (jnp.maximum(i - N_CTX_BLK, 0), 0))


def _read_rows(i, ctx_ref, lat_ref, rows=slice(None)):
    return jnp.where(i < N_CTX_BLK, ctx_ref[rows, :], lat_ref[rows, :])


def _write_rows(i, ctx_ref, lat_ref, value):
    @pl.when(i < N_CTX_BLK)
    def _():
        ctx_ref[...] = value

    @pl.when(i >= N_CTX_BLK)
    def _():
        lat_ref[...] = value


def _mods_kernel(c_ref, w_ref, b_ref, o_ref, sb_scr):
    @pl.when((pl.program_id(0) == 0) & (pl.program_id(1) == 0))
    def _():
        for v in range(N_GROUPS):
            sb_scr[v] = jnp.broadcast_to(_silu(c_ref[v]), (D_MODEL, LANES))

    n_tiles = MOD_TN // LANES

    def slab(r, accs):
        r0 = pl.multiple_of(r * SUBLANES, SUBLANES)
        s = [sb_scr[v, pl.ds(r0, SUBLANES), :] for v in range(N_GROUPS)]
        out = []
        for t in range(n_tiles):
            w = w_ref[pl.ds(r0, SUBLANES), t * LANES:(t + 1) * LANES]
            out += [accs[t * N_GROUPS + v] + w * s[v] for v in range(N_GROUPS)]
        return tuple(out)

    zero = jnp.zeros((SUBLANES, LANES), F32)
    accs = lax.fori_loop(0, D_MODEL // SUBLANES, slab, (zero,) * (n_tiles * N_GROUPS), unroll=8)
    for t in range(n_tiles):
        cols = slice(t * LANES, (t + 1) * LANES)
        for v in range(N_GROUPS):
            o_ref[v, :, cols] = jnp.sum(accs[t * N_GROUPS + v], axis=0, keepdims=True) + b_ref[:, cols]


def _mods(c_cols, w_mod, b_mod):
    n_out = 6 * D_MODEL
    return pl.pallas_call(
        _mods_kernel,
        grid=(DEPTH, n_out // MOD_TN),
        in_specs=[
            pl.BlockSpec((N_GROUPS, D_MODEL, 1), lambda l, n: (0, 0, 0)),
            pl.BlockSpec((None, D_MODEL, MOD_TN), lambda l, n: (l, 0, n)),
            pl.BlockSpec((None, 1, MOD_TN), lambda l, n: (l, 0, n)),
        ],
        out_specs=pl.BlockSpec((None, N_GROUPS, 1, MOD_TN), lambda l, n: (l, 0, 0, n)),
        out_shape=jax.ShapeDtypeStruct((DEPTH, N_GROUPS, 1, n_out), F32),
        scratch_shapes=[pltpu.VMEM((N_GROUPS, D_MODEL, LANES), F32)],
        compiler_params=_params(("arbitrary", "arbitrary")),
        name="adaln_mods",
    )(c_cols, w_mod, b_mod.reshape(DEPTH, 1, n_out))


def _cumsum2(t, g):
    g1 = g.astype(BF16)
    g2 = (g - g1.astype(F32)).astype(BF16)
    return _dot(t, g1) + _dot(t, g2)


def _rows_bcast(b, first_row):
    n = b.shape[0] // CHUNK
    return jnp.concatenate(
        [jnp.broadcast_to(b[c * CHUNK + first_row:c * CHUNK + first_row + 1, :], (CHUNK, b.shape[1]))
         for c in range(n)], axis=0)


def _gla_in_kernel(xc_ref, xl_ref, mod_ref, ng_ref, w_ref, wg2_ref, bg_ref, tlow_ref, tup_ref,
                   qf_ref, kf_ref, qb_ref, kb_ref, v_ref, r_ref, oi_ref, df_ref, db_ref):
    i = pl.program_id(0)
    ri = lax.broadcasted_iota(jnp.int32, (SUPER, SUPER), 0)
    ci = lax.broadcasted_iota(jnp.int32, (SUPER, SUPER), 1)
    same = (ri & -CHUNK) == (ci & -CHUNK)
    low = same & (ci <= ri)
    up = same & (ci >= ri)
    cpg = SUPER // CHUNK
    inv_tau = 1.0 / GATE_TAU

    def project(s, res):
        rows = slice(s * SUPER, (s + 1) * SUPER)
        x = _read_rows(i, xc_ref, xl_ref, rows)
        h = _rms(x, ng_ref[...]) * (1.0 + _mod_part(mod_ref, 1)) + _mod_part(mod_ref, 0)
        hb = h.astype(BF16)
        yield
        qk = _dot(hb, w_ref[:, 0:2 * GLA_QK])
        res["q"] = qk[:, 0:GLA_QK] * (GLA_DK ** -0.5)
        res["k"] = qk[:, GLA_QK:2 * GLA_QK]
        yield
        res["v"] = _dot(hb, w_ref[:, 2 * GLA_QK:2 * GLA_QK + GLA_V]).astype(BF16)
        v_ref[rows, :] = res["v"]
        yield
        r_ref[rows, :] = _dot(hb, w_ref[:, 2 * GLA_QK + GLA_V:GLR_COL])
        yield
        res["glr"] = _dot(hb, w_ref[:, GLR_COL:GLA_IN_PAD]).astype(BF16)

    def local(s, res):
        rows = slice(s * SUPER, (s + 1) * SUPER)
        qs, ks, vb, glr = res["q"], res["k"], res["v"], res["glr"]
        gf = _log_sigmoid(_dot(glr, wg2_ref[0]) + bg_ref[0:1, :]) * inv_tau
        gb = _log_sigmoid(_dot(glr, wg2_ref[1]) + bg_ref[1:2, :]) * inv_tau
        yield
        b = _cumsum2(tlow_ref[...], gf)
        bmid = _rows_bcast(b, CHUNK // 2 - 1)
        blast = _rows_bcast(b, CHUNK - 1)
        qtf = (qs * jnp.exp(b - bmid)).astype(BF16)
        ktf = (ks * jnp.exp(bmid - b)).astype(BF16)
        qf_ref[rows, :] = (qs * jnp.exp(b)).astype(BF16)
        kf_ref[rows, :] = (ks * jnp.exp(blast - b)).astype(BF16)
        for c in range(cpg):
            df_ref[s * cpg + c:s * cpg + c + 1, :] = jnp.exp(b[c * CHUNK + CHUNK - 1:(c + 1) * CHUNK, :])
        yield
        b = _cumsum2(tup_ref[...], gb)
        bmid = _rows_bcast(b, CHUNK // 2)
        blast = _rows_bcast(b, 0)
        qtb = (qs * jnp.exp(b - bmid)).astype(BF16)
        ktb = (ks * jnp.exp(bmid - b)).astype(BF16)
        qb_ref[rows, :] = (qs * jnp.exp(b)).astype(BF16)
        kb_ref[rows, :] = (ks * jnp.exp(blast - b)).astype(BF16)
        for c in range(cpg):
            db_ref[s * cpg + c:s * cpg + c + 1, :] = jnp.exp(b[c * CHUNK:c * CHUNK + 1, :])
        yield
        for hd in range(GLA_HEADS):
            kc = slice(hd * GLA_DK, (hd + 1) * GLA_DK)
            af = _dot_nt(qtf[:, kc], ktf[:, kc])
            ab = _dot_nt(qtb[:, kc], ktb[:, kc])
            a = (jnp.where(low, af, 0.0) + jnp.where(up, ab, 0.0)).astype(BF16)
            vc = slice(hd * GLA_DV, (hd + 1) * GLA_DV)
            oi_ref[rows, vc] = _dot(a, vb[:, vc])

    n_groups = TM // SUPER
    results = [dict() for _ in range(n_groups)]
    for s in range(n_groups + 1):
        stages = []
        if s < n_groups:
            stages.append(project(s, results[s]))
        if s > 0:
            stages.append(local(s - 1, results[s - 1]))
        while stages:
            stages = [g for g in stages if next(g, "done") != "done"]


def _gla_in(xc, xl, mods, layer, norm_g, w_in_pad, wg2_pad, b_g, tlow, tup):
    cpb = TM // CHUNK
    row = lambda w: pl.BlockSpec((TM, w), lambda i: (i, 0))
    outs = [
        jax.ShapeDtypeStruct((N_TOK, GLA_QK), BF16),
        jax.ShapeDtypeStruct((N_TOK, GLA_QK), BF16),
        jax.ShapeDtypeStruct((N_TOK, GLA_QK), BF16),
        jax.ShapeDtypeStruct((N_TOK, GLA_QK), BF16),
        jax.ShapeDtypeStruct((N_TOK, GLA_V), BF16),
        jax.ShapeDtypeStruct((N_TOK, GLA_V), F32),
        jax.ShapeDtypeStruct((N_TOK, GLA_V), F32),
        jax.ShapeDtypeStruct((N_TOK // CHUNK, GLA_QK), F32),
        jax.ShapeDtypeStruct((N_TOK // CHUNK, GLA_QK), F32),
    ]
    return pl.pallas_call(
        _gla_in_kernel,
        grid=(N_TOK // TM,),
        in_specs=[
            _ctx_rows(), _lat_rows(), _mod_spec(layer), _const_spec((1, D_MODEL)),
            _const_spec((D_MODEL, GLA_IN_PAD)), _const_spec((2, 256, GLA_QK)),
            _const_spec((2, GLA_QK)), _const_spec((SUPER, SUPER)), _const_spec((SUPER, SUPER)),
        ],
        out_specs=[row(GLA_QK)] * 4 + [row(GLA_V)] * 3
        + [pl.BlockSpec((cpb, GLA_QK), lambda i: (i, 0))] * 2,
        out_shape=outs,
        compiler_params=_params(("arbitrary",)),
        name="gla_in",
    )(xc, xl, mods, norm_g, w_in_pad, wg2_pad, b_g, tlow, tup)


def _scan_steps(q_ref, k_ref, d_ref, v_ref, chunks, st):
    rows = [pl.multiple_of(ch * CHUNK, CHUNK) for ch in chunks]
    updates = [_dot_tn(v_ref[pl.ds(r0, CHUNK), :], k_ref[pl.ds(r0, CHUNK), :]) for r0 in rows]
    reads = []
    for ch, r0, u in zip(chunks, rows, updates):
        reads.append((r0, _dot_nt(q_ref[pl.ds(r0, CHUNK), :], st.astype(BF16))))
        st = st * d_ref[pl.ds(ch, 1), :] + u
    return st, reads


def _gla_scan_kernel(has_prev, *refs):
    (qf_ref, kf_ref, qb_ref, kb_ref, v_ref, oi_ref, df_ref, db_ref, s0_ref) = refs[:9]
    refs = refs[9:]
    if has_prev:
        prev_ref, refs = refs[0], refs[1:]
    o_ref, st_ref, s_scr = refs
    i = pl.program_id(1)
    o_ref[...] = oi_ref[...]

    def accumulate(reads):
        for r0, val in reads:
            o_ref[pl.ds(r0, CHUNK), :] += val

    @pl.when(i < N_CTX_SCAN)
    def _():
        cps = SEQ // CHUNK
        seq_per_trip = SCAN_STEPS // cps
        layer_slot = N_GLA_LAYERS - 1 if has_prev else 0
        if has_prev:
            for l in range(N_GLA_LAYERS - 1):
                st_ref[:, l] = prev_ref[:, l]

        def trip(t, carry):
            for u in range(seq_per_trip):
                s = t * seq_per_trip + u
                zero = jnp.zeros((GLA_DV, GLA_DK), F32)
                sf, rf = _scan_steps(qf_ref, kf_ref, df_ref, v_ref,
                                     [s * cps + c for c in range(cps)], zero)
                sb, rb = _scan_steps(qb_ref, kb_ref, db_ref, v_ref,
                                     [s * cps + (cps - 1 - c) for c in range(cps)], zero)
                accumulate(rf + rb)
                st_ref[s, layer_slot, 0] = sf.T
                st_ref[s, layer_slot, 1] = sb.T
            return carry

        lax.fori_loop(0, SCAN_ROWS // SEQ // seq_per_trip, trip, 0)

    @pl.when(i >= N_CTX_SCAN)
    def _():
        n_chunks = SCAN_ROWS // CHUNK
        s_scr[0] = s0_ref[0].T
        s_scr[1] = s0_ref[1].T

        def trip(t, carry):
            first = t * SCAN_STEPS
            sf, rf = _scan_steps(qf_ref, kf_ref, df_ref, v_ref,
                                 [first + c for c in range(SCAN_STEPS)], s_scr[0])
            sb, rb = _scan_steps(qb_ref, kb_ref, db_ref, v_ref,
                                 [n_chunks - 1 - first - c for c in range(SCAN_STEPS)], s_scr[1])
            accumulate(rf + rb)
            s_scr[0] = sf
            s_scr[1] = sb
            return carry

        lax.fori_loop(0, n_chunks // SCAN_STEPS, trip, 0)


def _gla_scan(qf, kf, qb, kb, v, oi, df, db, state_gla, gla_layer, prev_states):
    has_prev = prev_states is not None
    n_layers_out = gla_layer + 1
    cpb = SCAN_ROWS // CHUNK
    seq_per_blk = SCAN_ROWS // SEQ
    qk_spec = pl.BlockSpec((SCAN_ROWS, GLA_DK), lambda h, i: (i, h))
    v_spec = pl.BlockSpec((SCAN_ROWS, GLA_DV), lambda h, i: (i, h))
    d_spec = pl.BlockSpec((cpb, GLA_DK), lambda h, i: (i, h))
    ctx_blk = lambda h, i: jnp.minimum(i, N_CTX_SCAN - 1)
    in_specs = [qk_spec, qk_spec, qk_spec, qk_spec, v_spec, v_spec, d_spec, d_spec,
                pl.BlockSpec((None, None, 2, None, GLA_DK, GLA_DV),
                             lambda h, i: (jnp.maximum(i - N_CTX_SCAN, 0), gla_layer, 0, h, 0, 0))]
    args = [qf, kf, qb, kb, v, oi, df, db, state_gla]
    if has_prev:
        in_specs.append(pl.BlockSpec((seq_per_blk, gla_layer, 2, None, GLA_DK, GLA_DV),
                                     lambda h, i: (ctx_blk(h, i), 0, 0, h, 0, 0)))
        args.append(prev_states)
    return pl.pallas_call(
        functools.partial(_gla_scan_kernel, has_prev),
        grid=(GLA_HEADS, N_TOK // SCAN_ROWS),
        in_specs=in_specs,
        out_specs=[
            v_spec,
            pl.BlockSpec((seq_per_blk, n_layers_out, 2, None, GLA_DK, GLA_DV),
                         lambda h, i: (ctx_blk(h, i), 0, 0, h, 0, 0)),
        ],
        out_shape=[jax.ShapeDtypeStruct((N_TOK, GLA_V), F32),
                   jax.ShapeDtypeStruct((BATCH, n_layers_out, 2, GLA_HEADS, GLA_DK, GLA_DV), F32)],
        scratch_shapes=[pltpu.VMEM((2, GLA_DV, GLA_DK), F32)],
        compiler_params=_params(("arbitrary", "arbitrary")),
        name="gla_scan",
    )(*args)


def _fnet_in_kernel(xc_ref, xl_ref, mod_ref, ng_ref, w_ref, cs_ref, c_ref, s_ref, uc_ref, us_ref):
    i = pl.program_id(0)
    x = _read_rows(i, xc_ref, xl_ref)
    h = _rms(x, ng_ref[...]) * (1.0 + _mod_part(mod_ref, 1)) + _mod_part(mod_ref, 0)
    u = _dot(h.astype(BF16), w_ref[...]).astype(BF16)
    for g in range(FNET_GROUPS):
        cols = slice(g * FNET_GW, (g + 1) * FNET_GW)
        t = _dot(u[:, cols], cs_ref[...])
        uc_ref[:, cols] = t[:, 0:FNET_GW].astype(BF16)
        us_ref[:, cols] = t[:, FNET_GW:2 * FNET_GW].astype(BF16)

    @pl.when(i < N_CTX_BLK)
    def _():
        for s in range(TM // SEQ):
            rows = slice(s * SEQ, (s + 1) * SEQ)
            f = _dot(c_ref[...], uc_ref[rows, :]) - _dot(s_ref[...], us_ref[rows, :])
            uc_ref[rows, :] = (f * SEQ ** -0.5).astype(BF16)


def _fnet_in(xc, xl, mods, layer, norm_g, w_in, cs_chan, c_tab, s_tab):
    row = pl.BlockSpec((TM, D_MODEL), lambda i: (i, 0))
    return pl.pallas_call(
        _fnet_in_kernel,
        grid=(N_TOK // TM,),
        in_specs=[_ctx_rows(), _lat_rows(), _mod_spec(layer), _const_spec((1, D_MODEL)),
                  _const_spec((D_MODEL, D_MODEL)), _const_spec((FNET_GW, 2 * FNET_GW)),
                  _const_spec((SEQ, SEQ)), _const_spec((SEQ, SEQ))],
        out_specs=[row, row],
        out_shape=[jax.ShapeDtypeStruct((N_TOK, D_MODEL), BF16)] * 2,
        compiler_params=_params(("arbitrary",)),
        name="fnet_in",
    )(xc, xl, mods, norm_g, w_in, cs_chan, c_tab, s_tab)


def _dft_lat_kernel(uc_ref, us_ref, c_ref, s_ref, f_ref):
    f = _dot(c_ref[...], uc_ref[...]) - _dot(s_ref[...], us_ref[...])
    f_ref[...] = (f * DEC_SEQ ** -0.5).astype(BF16)


def _fnet_dft_lat(uc, us, c_lat, s_lat):
    n_ctx_blk = N_CTX_TOK // DEC_SEQ
    m_tiles = DEC_SEQ // TM
    seq_blk = pl.BlockSpec((DEC_SEQ, D_MODEL), lambda b, m: (n_ctx_blk + b, 0))
    tab = pl.BlockSpec((TM, DEC_SEQ), lambda b, m: (m, 0))
    return pl.pallas_call(
        _dft_lat_kernel,
        grid=(DEC_BATCH, m_tiles),
        in_specs=[seq_blk, seq_blk, tab, tab],
        out_specs=pl.BlockSpec((TM, D_MODEL), lambda b, m: (b * m_tiles + m, 0)),
        out_shape=jax.ShapeDtypeStruct((N_LAT_TOK, D_MODEL), BF16),
        compiler_params=_params(("arbitrary", "arbitrary")),
        name="fnet_dft_lat",
    )(uc, us, c_lat, s_lat)


def _mix_ffn_kernel(is_gla, is_last, *refs):
    xc_ref, xl_ref = refs[:2]
    if is_gla:
        o_ref, r_ref, gn_ref = refs[2:5]
        refs = refs[5:]
    else:
        fc_ref, fl_ref = refs[2:4]
        refs = refs[4:]
    (wo_ref, mod_ref, ng_ref, wu_ref, cw_ref, cb_ref, wd_ref, fg_ref,
     outc_ref, outl_ref, x_scr, h_scr, acc_scr, out_scr) = refs
    i = pl.program_id(0)
    j = pl.program_id(1)
    n_f = FFN_DIM // TF
    period = jnp.where(i < N_CTX_BLK, SEQ, GRID_W)
    pos = lax.broadcasted_iota(jnp.int32, (HALF, 1), 0) & (period - 1)
    has_prev = (pos != 0).astype(F32)
    has_next = (pos != period - 1).astype(F32)

    def mixer_out(rows):
        if is_gla:
            parts = []
            for hd in range(GLA_HEADS):
                cols = slice(hd * GLA_DV, (hd + 1) * GLA_DV)
                y = _rms(o_ref[rows, cols], gn_ref[...]) * _silu(r_ref[rows, cols])
                parts.append(y.astype(BF16))
            y = jnp.concatenate(parts, axis=1)
        else:
            y = _read_rows(i, fc_ref, fl_ref, rows)
        x_new = _read_rows(i, xc_ref, xl_ref, rows) + _mod_part(mod_ref, 2) * _dot(y, wo_ref[...])
        x_scr[rows, :] = x_new
        h = _rms(x_new, ng_ref[...]) * (1.0 + _mod_part(mod_ref, 4)) + _mod_part(mod_ref, 3)
        hb = h.astype(BF16)
        h_scr[rows, :] = hb
        return hb

    def ffn_tile(hb):
        ag = _dot(hb, wu_ref[...])
        a = ag[:, 0:TF]
        g = ag[:, TF:2 * TF]
        a_prev = pltpu.roll(a, 1, 0) * has_prev
        a_next = pltpu.roll(a, HALF - 1, 0) * has_next
        conv = a_prev * cw_ref[0:1, :] + a * cw_ref[1:2, :] + a_next * cw_ref[2:3, :] + cb_ref[...]
        return _dot((_silu(conv) * g).astype(BF16), wd_ref[...])

    def step(first, last):
        for s in range(TM // HALF):
            rows = slice(s * HALF, (s + 1) * HALF)
            hb = mixer_out(rows) if first else h_scr[rows, :]
            d = ffn_tile(hb)
            if not first:
                d = acc_scr[rows, :] + d
            if last:
                out = x_scr[rows, :] + _mod_part(mod_ref, 5) * d
                if is_last:
                    out = _rms(out, fg_ref[...])
                out_scr[rows, :] = out
            else:
                acc_scr[rows, :] = d

    pl.when(j == 0)(lambda: step(True, n_f == 1))
    if n_f > 2:
        pl.when((j > 0) & (j < n_f - 1))(lambda: step(False, False))
    if n_f > 1:
        pl.when(j == n_f - 1)(lambda: step(False, True))

    @pl.when(j == n_f - 1)
    def _():
        _write_rows(i, outc_ref, outl_ref, out_scr[...])


def _mix_ffn(is_gla, is_last, xc, xl, mix_in, w_o, mods, layer, norm_g, w_up_tiles, conv_w, conv_b,
             w_down, final_g):
    row = pl.BlockSpec((TM, D_MODEL), lambda i, j: (i, 0))
    if is_gla:
        mix_specs = [row, row, _const_spec((1, GLA_DV))]
    else:
        mix_specs = [_ctx_rows(), _lat_rows()]
    in_specs = [_ctx_rows(), _lat_rows()] + mix_specs + [
        _const_spec((D_MODEL, D_MODEL)), _mod_spec(layer), _const_spec((1, D_MODEL)),
        pl.BlockSpec((D_MODEL, 2 * TF), lambda i, j: (0, j)),
        pl.BlockSpec((3, TF), lambda i, j: (0, j)),
        pl.BlockSpec((1, TF), lambda i, j: (0, j)),
        pl.BlockSpec((TF, D_MODEL), lambda i, j: (j, 0)),
        _const_spec((1, D_MODEL)),
    ]
    return pl.pallas_call(
        functools.partial(_mix_ffn_kernel, is_gla, is_last),
        grid=(N_TOK // TM, FFN_DIM // TF),
        in_specs=in_specs,
        out_specs=[_ctx_rows(), _lat_rows()],
        out_shape=[jax.ShapeDtypeStruct((N_CTX_TOK, D_MODEL), F32),
                   jax.ShapeDtypeStruct((N_LAT_TOK, D_MODEL), F32)],
        scratch_shapes=[pltpu.VMEM((TM, D_MODEL), F32), pltpu.VMEM((TM, D_MODEL), BF16),
                        pltpu.VMEM((TM, D_MODEL), F32), pltpu.VMEM((TM, D_MODEL), F32)],
        compiler_params=_params(("arbitrary", "arbitrary")),
        name="gla_out_ffn" if is_gla else "fnet_out_ffn",
    )(xc, xl, *mix_in, w_o, mods, norm_g, w_up_tiles, conv_w, conv_b, w_down, final_g)


def _dft_tables(n, scale):
    idx = np.arange(n, dtype=np.int64)
    ang = (2.0 * np.pi / n) * ((idx[:, None] * idx[None, :]) % n).astype(np.float64)
    return (np.cos(ang) * scale).astype(np.float32), (np.sin(ang) * scale).astype(np.float32)


def _chunk_triangles():
    idx = np.arange(SUPER)
    same = (idx[:, None] // CHUNK) == (idx[None, :] // CHUNK)
    low = same & (idx[None, :] <= idx[:, None])
    up = same & (idx[None, :] >= idx[:, None])
    return low.astype(np.float32), up.astype(np.float32)


def kernel(x_prompt, x_sample, state_gla, c, c_ctx, norm_mix_g, norm_ffn_g, w_mod, b_mod,
           gla_w_in, gla_w_g2, gla_b_g, gla_norm_g, gla_w_o, fnet_w_in, fnet_w_o,
           ffn_w_up, ffn_conv_w, ffn_conv_b, ffn_w_down, final_norm_g):
    xc = x_prompt.reshape(N_CTX_TOK, D_MODEL)
    xl = x_sample.reshape(N_LAT_TOK, D_MODEL)
    c_cols = jnp.concatenate([c_ctx[None, :], c], axis=0)[:, :, None]
    mods = _mods(c_cols, w_mod, b_mod)

    tlow_np, tup_np = _chunk_triangles()
    tlow, tup = jnp.asarray(tlow_np).astype(BF16), jnp.asarray(tup_np).astype(BF16)
    cc, sc = _dft_tables(FNET_GW, FNET_GW ** -0.5)
    cs_chan = jnp.asarray(np.concatenate([cc, sc], axis=1)).astype(BF16)
    c_ctx_tab, s_ctx_tab = (jnp.asarray(t).astype(BF16) for t in _dft_tables(SEQ, 1.0))
    c_lat_tab, s_lat_tab = (jnp.asarray(t).astype(BF16) for t in _dft_tables(DEC_SEQ, 1.0))
    final_g = final_norm_g.reshape(1, D_MODEL)

    n_f = FFN_DIM // TF
    w_up_tiles = ffn_w_up.reshape(DEPTH, D_MODEL, 2, n_f, TF).transpose(0, 1, 3, 2, 4)
    w_up_tiles = w_up_tiles.reshape(DEPTH, D_MODEL, 2 * FFN_DIM).astype(BF16)
    w_down = ffn_w_down.astype(BF16)

    states = None
    for layer in range(DEPTH):
        j = layer // 2
        ng_mix = norm_mix_g[layer].reshape(1, D_MODEL)
        if layer % 2 == 0:
            w_in_pad = jnp.pad(gla_w_in[j], ((0, 0), (0, GLA_IN_PAD - GLA_IN))).astype(BF16)
            wg2_pad = jnp.zeros((2, 256, GLA_QK), F32)
            wg2_pad = wg2_pad.at[0, 0:GATE_RANK].set(gla_w_g2[j, 0])
            wg2_pad = wg2_pad.at[1, GATE_RANK:2 * GATE_RANK].set(gla_w_g2[j, 1]).astype(BF16)
            qf, kf, qb, kb, v, r, oi, df, db = _gla_in(
                xc, xl, mods, layer, ng_mix, w_in_pad, wg2_pad, gla_b_g[j], tlow, tup)
            o, states = _gla_scan(qf, kf, qb, kb, v, oi, df, db, state_gla, j, states)
            mix_in = (o, r, gla_norm_g[j].reshape(1, GLA_DV))
            w_o = gla_w_o[j]
        else:
            uc, us = _fnet_in(xc, xl, mods, layer, ng_mix, fnet_w_in[j].astype(BF16), cs_chan,
                              c_ctx_tab, s_ctx_tab)
            f_lat = _fnet_dft_lat(uc, us, c_lat_tab, s_lat_tab)
            mix_in = (uc, f_lat)
            w_o = fnet_w_o[j]
        xc, xl = _mix_ffn(layer % 2 == 0, layer == DEPTH - 1, xc, xl, mix_in, w_o.astype(BF16), mods,
                          layer, norm_ffn_g[layer].reshape(1, D_MODEL), w_up_tiles[layer],
                          ffn_conv_w[layer], ffn_conv_b[layer].reshape(1, FFN_DIM), w_down[layer],
                          final_g)

    return (xc.reshape(BATCH, SEQ, D_MODEL), xl.reshape(DEC_BATCH, DEC_SEQ, D_MODEL), states)
```

```python
import functools

import numpy as np
import jax
import jax.numpy as jnp
from jax import lax
from jax.experimental import pallas as pl
from jax.experimental.pallas import tpu as pltpu

F32 = jnp.float32
BF16 = jnp.bfloat16

D_MODEL = 1024
BATCH = 16
SEQ = 256
DEPTH = 4
DEC_BATCH = 2
DEC_SEQ = 2048
GRID_W = 64
N_GLA_LAYERS = 2
GLA_HEADS = 4
GLA_QK = D_MODEL // 2
GLA_V = D_MODEL
GLA_DK = GLA_QK // GLA_HEADS
GLA_DV = GLA_V // GLA_HEADS
GATE_RANK = 16
GATE_TAU = 16.0
GLA_IN = 2 * GLA_QK + 2 * GLA_V + 2 * GATE_RANK
FNET_GROUPS = 4
FNET_GW = D_MODEL // FNET_GROUPS
FFN_DIM = 2816
EPS = 1e-6

N_CTX_TOK = BATCH * SEQ
N_LAT_TOK = DEC_BATCH * DEC_SEQ
N_TOK = N_CTX_TOK + N_LAT_TOK
N_GROUPS = 1 + DEC_BATCH

LANES = 128
SUBLANES = 8
CHUNK = 64
SUPER = 256
TM = 512
N_CTX_BLK = N_CTX_TOK // TM
SCAN_ROWS = 2048
N_CTX_SCAN = N_CTX_TOK // SCAN_ROWS
SCAN_STEPS = 8
HALF = 256
TF = 1408
GLA_IN_PAD = 3328
GLR_COL = 2 * GLA_QK + 2 * GLA_V
VMEM_LIMIT = 56 * 1024 * 1024
MOD_TN = 1024


def _dot(a, b):
    return jnp.dot(a, b, preferred_element_type=F32)


def _dot_nt(a, b):
    return lax.dot_general(a, b, (((1,), (1,)), ((), ())), preferred_element_type=F32)


def _dot_tn(a, b):
    return lax.dot_general(a, b, (((0,), (0,)), ((), ())), preferred_element_type=F32)


def _silu(x):
    return x / (1.0 + jnp.exp(-x))


def _log_sigmoid(x):
    return jnp.minimum(x, 0.0) - jnp.log(1.0 + jnp.exp(-jnp.abs(x)))


def _rms(x, g):
    return x * lax.rsqrt(jnp.mean(x * x, axis=-1, keepdims=True) + EPS) * g


def _group_of_block(i):
    return jnp.where(i < N_CTX_BLK, 0, 1 + (i - N_CTX_BLK) // (DEC_SEQ // TM))


def _params(sem):
    return pltpu.CompilerParams(dimension_semantics=sem, vmem_limit_bytes=VMEM_LIMIT)


def _const_spec(shape):
    nd = len(shape)
    return pl.BlockSpec(shape, lambda *_: (0,) * nd, pipeline_mode=pl.Buffered(1))


def _layer_spec(shape, layer):
    nd = len(shape)
    return pl.BlockSpec((None,) + tuple(shape), lambda *_: (layer,) + (0,) * nd,
                        pipeline_mode=pl.Buffered(1))


def _mod_spec(layer):
    return pl.BlockSpec((None, None, 1, 6 * D_MODEL),
                        lambda i, *_: (layer, _group_of_block(i), 0, 0))


def _mod_part(mod_ref, k):
    return mod_ref[:, k * D_MODEL:(k + 1) * D_MODEL]


def _ctx_rows(width=D_MODEL):
    return pl.BlockSpec((TM, width), lambda i, *_: (jnp.minimum(i, N_CTX_BLK - 1), 0))


def _lat_rows(width=D_MODEL):
    return pl.BlockSpec((TM, width), lambda i, *_: (jnp.maximum(i - N_CTX_BLK, 0), 0))


def _read_rows(i, ctx_ref, lat_ref, rows=slice(None)):
    return jnp.where(i < N_CTX_BLK, ctx_ref[rows, :], lat_ref[rows, :])


def _write_rows(i, ctx_ref, lat_ref, value):
    @pl.when(i < N_CTX_BLK)
    def _():
        ctx_ref[...] = value

    @pl.when(i >= N_CTX_BLK)
    def _():
        lat_ref[...] = value


def _mods_kernel(c_ref, w_ref, b_ref, o_ref, sb_scr):
    @pl.when((pl.program_id(0) == 0) & (pl.program_id(1) == 0))
    def _():
        for v in range(N_GROUPS):
            sb_scr[v] = jnp.broadcast_to(_silu(c_ref[v]), (D_MODEL, LANES))

    n_tiles = MOD_TN // LANES

    def slab(r, accs):
        r0 = pl.multiple_of(r * SUBLANES, SUBLANES)
        s = [sb_scr[v, pl.ds(r0, SUBLANES), :] for v in range(N_GROUPS)]
        out = []
        for t in range(n_tiles):
            w = w_ref[pl.ds(r0, SUBLANES), t * LANES:(t + 1) * LANES]
            out += [accs[t * N_GROUPS + v] + w * s[v] for v in range(N_GROUPS)]
        return tuple(out)

    zero = jnp.zeros((SUBLANES, LANES), F32)
    accs = lax.fori_loop(0, D_MODEL // SUBLANES, slab, (zero,) * (n_tiles * N_GROUPS), unroll=8)
    for t in range(n_tiles):
        cols = slice(t * LANES, (t + 1) * LANES)
        for v in range(N_GROUPS):
            o_ref[v, :, cols] = jnp.sum(accs[t * N_GROUPS + v], axis=0, keepdims=True) + b_ref[:, cols]


def _mods(c_cols, w_mod, b_mod):
    n_out = 6 * D_MODEL
    return pl.pallas_call(
        _mods_kernel,
        grid=(DEPTH, n_out // MOD_TN),
        in_specs=[
            pl.BlockSpec((N_GROUPS, D_MODEL, 1), lambda l, n: (0, 0, 0)),
            pl.BlockSpec((None, D_MODEL, MOD_TN), lambda l, n: (l, 0, n)),
            pl.BlockSpec((None, 1, MOD_TN), lambda l, n: (l, 0, n)),
        ],
        out_specs=pl.BlockSpec((None, N_GROUPS, 1, MOD_TN), lambda l, n: (l, 0, 0, n)),
        out_shape=jax.ShapeDtypeStruct((DEPTH, N_GROUPS, 1, n_out), F32),
        scratch_shapes=[pltpu.VMEM((N_GROUPS, D_MODEL, LANES), F32)],
        compiler_params=_params(("arbitrary", "arbitrary")),
        name="adaln_mods",
    )(c_cols, w_mod, b_mod.reshape(DEPTH, 1, n_out))


def _cumsum2(t, g):
    g1 = g.astype(BF16)
    g2 = (g - g1.astype(F32)).astype(BF16)
    return _dot(t, g1) + _dot(t, g2)


def _rows_bcast(b, first_row):
    n = b.shape[0] // CHUNK
    return jnp.concatenate(
        [jnp.broadcast_to(b[c * CHUNK + first_row:c * CHUNK + first_row + 1, :], (CHUNK, b.shape[1]))
         for c in range(n)], axis=0)


def _gla_in_kernel(xc_ref, xl_ref, mod_ref, ng_ref, w32_ref, wg32_ref, bg_ref, tlow_ref, tup_ref,
                   qf_ref, kf_ref, qb_ref, kb_ref, v_ref, r_ref, oi_ref, df_ref, db_ref,
                   w_ref, wg2_ref):
    i = pl.program_id(0)

    @pl.when(i == 0)
    def _():
        w_ref[:, 0:GLR_COL] = w32_ref[:, 0:GLR_COL].astype(BF16)
        w_ref[:, GLR_COL:GLA_IN_PAD] = jnp.zeros((D_MODEL, GLA_IN_PAD - GLR_COL), BF16)
        w_ref[:, GLR_COL:GLA_IN] = w32_ref[:, GLR_COL:GLA_IN].astype(BF16)
        wg2_ref[...] = jnp.zeros_like(wg2_ref)
        for z in range(2):
            wg2_ref[z, z * GATE_RANK:(z + 1) * GATE_RANK, :] = wg32_ref[z].astype(BF16)

    ri = lax.broadcasted_iota(jnp.int32, (SUPER, SUPER), 0)
    ci = lax.broadcasted_iota(jnp.int32, (SUPER, SUPER), 1)
    same = (ri & -CHUNK) == (ci & -CHUNK)
    low = same & (ci <= ri)
    up = same & (ci >= ri)
    cpg = SUPER // CHUNK
    inv_tau = 1.0 / GATE_TAU

    def project(s, res):
        rows = slice(s * SUPER, (s + 1) * SUPER)
        x = _read_rows(i, xc_ref, xl_ref, rows)
        h = _rms(x, ng_ref[...]) * (1.0 + _mod_part(mod_ref, 1)) + _mod_part(mod_ref, 0)
        hb = h.astype(BF16)
        yield
        qk = _dot(hb, w_ref[:, 0:2 * GLA_QK])
        res["q"] = qk[:, 0:GLA_QK] * (GLA_DK ** -0.5)
        res["k"] = qk[:, GLA_QK:2 * GLA_QK]
        yield
        res["v"] = _dot(hb, w_ref[:, 2 * GLA_QK:2 * GLA_QK + GLA_V]).astype(BF16)
        v_ref[rows, :] = res["v"]
        yield
        r_ref[rows, :] = _dot(hb, w_ref[:, 2 * GLA_QK + GLA_V:GLR_COL])
        yield
        res["glr"] = _dot(hb, w_ref[:, GLR_COL:GLA_IN_PAD]).astype(BF16)

    def local(s, res):
        rows = slice(s * SUPER, (s + 1) * SUPER)
        qs, ks, vb, glr = res["q"], res["k"], res["v"], res["glr"]
        gf = _log_sigmoid(_dot(glr, wg2_ref[0]) + bg_ref[0:1, :]) * inv_tau
        gb = _log_sigmoid(_dot(glr, wg2_ref[1]) + bg_ref[1:2, :]) * inv_tau
        yield
        b = _cumsum2(tlow_ref[...], gf)
        bmid = _rows_bcast(b, CHUNK // 2 - 1)
        blast = _rows_bcast(b, CHUNK - 1)
        qtf = (qs * jnp.exp(b - bmid)).astype(BF16)
        ktf = (ks * jnp.exp(bmid - b)).astype(BF16)
        qf_ref[rows, :] = (qs * jnp.exp(b)).astype(BF16)
        kf_ref[rows, :] = (ks * jnp.exp(blast - b)).astype(BF16)
        for c in range(cpg):
            df_ref[s * cpg + c:s * cpg + c + 1, :] = jnp.exp(b[c * CHUNK + CHUNK - 1:(c + 1) * CHUNK, :])
        yield
        b = _cumsum2(tup_ref[...], gb)
        bmid = _rows_bcast(b, CHUNK // 2)
        blast = _rows_bcast(b, 0)
        qtb = (qs * jnp.exp(b - bmid)).astype(BF16)
        ktb = (ks * jnp.exp(bmid - b)).astype(BF16)
        qb_ref[rows, :] = (qs * jnp.exp(b)).astype(BF16)
        kb_ref[rows, :] = (ks * jnp.exp(blast - b)).astype(BF16)
        for c in range(cpg):
            db_ref[s * cpg + c:s * cpg + c + 1, :] = jnp.exp(b[c * CHUNK:c * CHUNK + 1, :])
        yield
        for hd in range(GLA_HEADS):
            kc = slice(hd * GLA_DK, (hd + 1) * GLA_DK)
            af = _dot_nt(qtf[:, kc], ktf[:, kc])
            ab = _dot_nt(qtb[:, kc], ktb[:, kc])
            a = (jnp.where(low, af, 0.0) + jnp.where(up, ab, 0.0)).astype(BF16)
            vc = slice(hd * GLA_DV, (hd + 1) * GLA_DV)
            oi_ref[rows, vc] = _dot(a, vb[:, vc])

    n_groups = TM // SUPER
    results = [dict() for _ in range(n_groups)]
    for s in range(n_groups + 1):
        stages = []
        if s < n_groups:
            stages.append(project(s, results[s]))
        if s > 0:
            stages.append(local(s - 1, results[s - 1]))
        while stages:
            stages = [g for g in stages if next(g, "done") != "done"]


def _gla_in(xc, xl, mods, layer, norm_g, w_in, w_g2, b_g, tlow, tup):
    gla_layer = layer // 2
    cpb = TM // CHUNK
    row = lambda w: pl.BlockSpec((TM, w), lambda i: (i, 0))
    outs = [
        jax.ShapeDtypeStruct((N_TOK, GLA_QK), BF16),
        jax.ShapeDtypeStruct((N_TOK, GLA_QK), BF16),
        jax.ShapeDtypeStruct((N_TOK, GLA_QK), BF16),
        jax.ShapeDtypeStruct((N_TOK, GLA_QK), BF16),
        jax.ShapeDtypeStruct((N_TOK, GLA_V), BF16),
        jax.ShapeDtypeStruct((N_TOK, GLA_V), F32),
        jax.ShapeDtypeStruct((N_TOK, GLA_V), F32),
        jax.ShapeDtypeStruct((N_TOK // CHUNK, GLA_QK), F32),
        jax.ShapeDtypeStruct((N_TOK // CHUNK, GLA_QK), F32),
    ]
    return pl.pallas_call(
        _gla_in_kernel,
        grid=(N_TOK // TM,),
        in_specs=[
            _ctx_rows(), _lat_rows(), _mod_spec(layer), _layer_spec((1, D_MODEL), layer),
            _layer_spec((D_MODEL, GLA_IN), gla_layer),
            _layer_spec((2, GATE_RANK, GLA_QK), gla_layer), _layer_spec((2, GLA_QK), gla_layer),
            _const_spec((SUPER, SUPER)), _const_spec((SUPER, SUPER)),
        ],
        out_specs=[row(GLA_QK)] * 4 + [row(GLA_V)] * 3
        + [pl.BlockSpec((cpb, GLA_QK), lambda i: (i, 0))] * 2,
        out_shape=outs,
        scratch_shapes=[pltpu.VMEM((D_MODEL, GLA_IN_PAD), BF16),
                        pltpu.VMEM((2, GLA_IN_PAD - GLR_COL, GLA_QK), BF16)],
        compiler_params=_params(("arbitrary",)),
        name="gla_in",
    )(xc, xl, mods, norm_g.reshape(DEPTH, 1, D_MODEL), w_in, w_g2, b_g, tlow, tup)


def _scan_steps(q_ref, k_ref, d_ref, v_ref, chunks, st):
    rows = [pl.multiple_of(ch * CHUNK, CHUNK) for ch in chunks]
    updates = [_dot_tn(v_ref[pl.ds(r0, CHUNK), :], k_ref[pl.ds(r0, CHUNK), :]) for r0 in rows]
    reads = []
    for ch, r0, u in zip(chunks, rows, updates):
        reads.append((r0, _dot_nt(q_ref[pl.ds(r0, CHUNK), :], st.astype(BF16))))
        st = st * d_ref[pl.ds(ch, 1), :] + u
    return st, reads


def _gla_scan_kernel(has_prev, *refs):
    (qf_ref, kf_ref, qb_ref, kb_ref, v_ref, oi_ref, df_ref, db_ref, s0_ref) = refs[:9]
    refs = refs[9:]
    if has_prev:
        prev_ref, refs = refs[0], refs[1:]
    o_ref, st_ref, s_scr = refs
    i = pl.program_id(1)
    o_ref[...] = oi_ref[...]

    def accumulate(reads):
        for r0, val in reads:
            o_ref[pl.ds(r0, CHUNK), :] += val

    @pl.when(i < N_CTX_SCAN)
    def _():
        cps = SEQ // CHUNK
        seq_per_trip = SCAN_STEPS // cps
        layer_slot = N_GLA_LAYERS - 1 if has_prev else 0
        if has_prev:
            for l in range(N_GLA_LAYERS - 1):
                st_ref[:, l] = prev_ref[:, l]

        def trip(t, carry):
            for u in range(seq_per_trip):
                s = t * seq_per_trip + u
                zero = jnp.zeros((GLA_DV, GLA_DK), F32)
                sf, rf = _scan_steps(qf_ref, kf_ref, df_ref, v_ref,
                                     [s * cps + c for c in range(cps)], zero)
                sb, rb = _scan_steps(qb_ref, kb_ref, db_ref, v_ref,
                                     [s * cps + (cps - 1 - c) for c in range(cps)], zero)
                accumulate(rf + rb)
                st_ref[s, layer_slot, 0] = sf.T
                st_ref[s, layer_slot, 1] = sb.T
            return carry

        lax.fori_loop(0, SCAN_ROWS // SEQ // seq_per_trip, trip, 0)

    @pl.when(i >= N_CTX_SCAN)
    def _():
        n_chunks = SCAN_ROWS // CHUNK
        s_scr[0] = s0_ref[0].T
        s_scr[1] = s0_ref[1].T

        def trip(t, carry):
            first = t * SCAN_STEPS
            sf, rf = _scan_steps(qf_ref, kf_ref, df_ref, v_ref,
                                 [first + c for c in range(SCAN_STEPS)], s_scr[0])
            sb, rb = _scan_steps(qb_ref, kb_ref, db_ref, v_ref,
                                 [n_chunks - 1 - first - c for c in range(SCAN_STEPS)], s_scr[1])
            accumulate(rf + rb)
            s_scr[0] = sf
            s_scr[1] = sb
            return carry

        lax.fori_loop(0, n_chunks // SCAN_STEPS, trip, 0)


def _gla_scan(qf, kf, qb, kb, v, oi, df, db, state_gla, gla_layer, prev_states):
    has_prev = prev_states is not None
    n_layers_out = gla_layer + 1
    cpb = SCAN_ROWS // CHUNK
    seq_per_blk = SCAN_ROWS // SEQ
    qk_spec = pl.BlockSpec((SCAN_ROWS, GLA_DK), lambda h, i: (i, h))
    v_spec = pl.BlockSpec((SCAN_ROWS, GLA_DV), lambda h, i: (i, h))
    d_spec = pl.BlockSpec((cpb, GLA_DK), lambda h, i: (i, h))
    ctx_blk = lambda h, i: jnp.minimum(i, N_CTX_SCAN - 1)
    in_specs = [qk_spec, qk_spec, qk_spec, qk_spec, v_spec, v_spec, d_spec, d_spec,
                pl.BlockSpec((None, None, 2, None, GLA_DK, GLA_DV),
                             lambda h, i: (jnp.maximum(i - N_CTX_SCAN, 0), gla_layer, 0, h, 0, 0))]
    args = [qf, kf, qb, kb, v, oi, df, db, state_gla]
    if has_prev:
        in_specs.append(pl.BlockSpec((seq_per_blk, gla_layer, 2, None, GLA_DK, GLA_DV),
                                     lambda h, i: (ctx_blk(h, i), 0, 0, h, 0, 0)))
        args.append(prev_states)
    return pl.pallas_call(
        functools.partial(_gla_scan_kernel, has_prev),
        grid=(GLA_HEADS, N_TOK // SCAN_ROWS),
        in_specs=in_specs,
        out_specs=[
            v_spec,
            pl.BlockSpec((seq_per_blk, n_layers_out, 2, None, GLA_DK, GLA_DV),
                         lambda h, i: (ctx_blk(h, i), 0, 0, h, 0, 0)),
        ],
        out_shape=[jax.ShapeDtypeStruct((N_TOK, GLA_V), F32),
                   jax.ShapeDtypeStruct((BATCH, n_layers_out, 2, GLA_HEADS, GLA_DK, GLA_DV), F32)],
        scratch_shapes=[pltpu.VMEM((2, GLA_DV, GLA_DK), F32)],
        compiler_params=_params(("arbitrary", "arbitrary")),
        name="gla_scan",
    )(*args)


def _fnet_in_kernel(xc_ref, xl_ref, mod_ref, ng_ref, w32_ref, cs_ref, c_ref, s_ref, uc_ref, us_ref,
                    w_ref):
    i = pl.program_id(0)

    @pl.when(i == 0)
    def _():
        w_ref[...] = w32_ref[...].astype(BF16)

    x = _read_rows(i, xc_ref, xl_ref)
    h = _rms(x, ng_ref[...]) * (1.0 + _mod_part(mod_ref, 1)) + _mod_part(mod_ref, 0)
    u = _dot(h.astype(BF16), w_ref[...]).astype(BF16)
    for g in range(FNET_GROUPS):
        cols = slice(g * FNET_GW, (g + 1) * FNET_GW)
        t = _dot(u[:, cols], cs_ref[...])
        uc_ref[:, cols] = t[:, 0:FNET_GW].astype(BF16)
        us_ref[:, cols] = t[:, FNET_GW:2 * FNET_GW].astype(BF16)

    @pl.when(i < N_CTX_BLK)
    def _():
        for s in range(TM // SEQ):
            rows = slice(s * SEQ, (s + 1) * SEQ)
            f = _dot(c_ref[...], uc_ref[rows, :]) - _dot(s_ref[...], us_ref[rows, :])
            uc_ref[rows, :] = (f * SEQ ** -0.5).astype(BF16)


def _fnet_in(xc, xl, mods, layer, norm_g, w_in, cs_chan, c_tab, s_tab):
    row = pl.BlockSpec((TM, D_MODEL), lambda i: (i, 0))
    return pl.pallas_call(
        _fnet_in_kernel,
        grid=(N_TOK // TM,),
        in_specs=[_ctx_rows(), _lat_rows(), _mod_spec(layer), _layer_spec((1, D_MODEL), layer),
                  _layer_spec((D_MODEL, D_MODEL), layer // 2), _const_spec((FNET_GW, 2 * FNET_GW)),
                  _const_spec((SEQ, SEQ)), _const_spec((SEQ, SEQ))],
        out_specs=[row, row],
        out_shape=[jax.ShapeDtypeStruct((N_TOK, D_MODEL), BF16)] * 2,
        scratch_shapes=[pltpu.VMEM((D_MODEL, D_MODEL), BF16)],
        compiler_params=_params(("arbitrary",)),
        name="fnet_in",
    )(xc, xl, mods, norm_g.reshape(DEPTH, 1, D_MODEL), w_in, cs_chan, c_tab, s_tab)


HALF_SEQ = DEC_SEQ // 2


def _dft_lat_kernel(uc_ref, us_ref, c_ref, s_ref, tw_ref, f_ref, ec_scr, es_scr, oc_scr, os_scr):
    @pl.when(pl.program_id(1) == 0)
    def _():
        lo, hi = slice(0, HALF_SEQ), slice(HALF_SEQ, DEC_SEQ)
        c_lo, c_hi = uc_ref[lo, :].astype(F32), uc_ref[hi, :].astype(F32)
        s_lo, s_hi = us_ref[lo, :].astype(F32), us_ref[hi, :].astype(F32)
        ec_scr[...] = (c_lo + c_hi).astype(BF16)
        es_scr[...] = (s_lo + s_hi).astype(BF16)
        dc, ds = c_lo - c_hi, s_lo - s_hi
        cos_l, sin_l = tw_ref[:, 0:1], tw_ref[:, 1:2]
        oc_scr[...] = (cos_l * dc - sin_l * ds).astype(BF16)
        os_scr[...] = (sin_l * dc + cos_l * ds).astype(BF16)

    scale = DEC_SEQ ** -0.5
    even = _dot(c_ref[...], ec_scr[...]) - _dot(s_ref[...], es_scr[...])
    f_ref[:, 0:D_MODEL] = (even * scale).astype(BF16)
    odd = _dot(c_ref[...], oc_scr[...]) - _dot(s_ref[...], os_scr[...])
    f_ref[:, D_MODEL:2 * D_MODEL] = (odd * scale).astype(BF16)


def _fnet_dft_lat(uc, us, c_half, s_half, twiddle):
    n_ctx_blk = N_CTX_TOK // DEC_SEQ
    m_tiles = HALF_SEQ // TM
    seq_blk = pl.BlockSpec((DEC_SEQ, D_MODEL), lambda b, m: (n_ctx_blk + b, 0))
    tab = pl.BlockSpec((TM, HALF_SEQ), lambda b, m: (m, 0))
    pairs = pl.pallas_call(
        _dft_lat_kernel,
        grid=(DEC_BATCH, m_tiles),
        in_specs=[seq_blk, seq_blk, tab, tab, _const_spec((HALF_SEQ, 2))],
        out_specs=pl.BlockSpec((TM, 2 * D_MODEL), lambda b, m: (b * m_tiles + m, 0)),
        out_shape=jax.ShapeDtypeStruct((N_LAT_TOK // 2, 2 * D_MODEL), BF16),
        scratch_shapes=[pltpu.VMEM((HALF_SEQ, D_MODEL), BF16)] * 4,
        compiler_params=_params(("arbitrary", "arbitrary")),
        name="fnet_dft_lat",
    )(uc, us, c_half, s_half, twiddle)
    return pairs.reshape(N_LAT_TOK, D_MODEL)


def _mix_ffn_kernel(is_gla, is_last, *refs):
    xc_ref, xl_ref = refs[:2]
    if is_gla:
        o_ref, r_ref, gn_ref = refs[2:5]
        refs = refs[5:]
    else:
        fc_ref, fl_ref = refs[2:4]
        refs = refs[4:]
    (wo32_ref, mod_ref, ng_ref, wu_ref, cw_ref, cb_ref, wd_ref, fg_ref,
     outc_ref, outl_ref, x_scr, h_scr, acc_scr, out_scr, wo_ref) = refs
    i = pl.program_id(0)
    j = pl.program_id(1)
    n_f = FFN_DIM // TF

    @pl.when((i == 0) & (j == 0))
    def _():
        wo_ref[...] = wo32_ref[...].astype(BF16)

    period = jnp.where(i < N_CTX_BLK, SEQ, GRID_W)
    pos = lax.broadcasted_iota(jnp.int32, (HALF, 1), 0) & (period - 1)
    has_prev = (pos != 0).astype(F32)
    has_next = (pos != period - 1).astype(F32)

    def mixer_out(rows):
        if is_gla:
            parts = []
            for hd in range(GLA_HEADS):
                cols = slice(hd * GLA_DV, (hd + 1) * GLA_DV)
                y = _rms(o_ref[rows, cols], gn_ref[...]) * _silu(r_ref[rows, cols])
                parts.append(y.astype(BF16))
            y = jnp.concatenate(parts, axis=1)
        else:
            y = _read_rows(i, fc_ref, fl_ref, rows)
        x_new = _read_rows(i, xc_ref, xl_ref, rows) + _mod_part(mod_ref, 2) * _dot(y, wo_ref[...])
        x_scr[rows, :] = x_new
        h = _rms(x_new, ng_ref[...]) * (1.0 + _mod_part(mod_ref, 4)) + _mod_part(mod_ref, 3)
        hb = h.astype(BF16)
        h_scr[rows, :] = hb
        return hb

    def ffn_tile(hb):
        ag = _dot(hb, wu_ref[...])
        a = ag[:, 0:TF]
        g = ag[:, TF:2 * TF]
        a_prev = pltpu.roll(a, 1, 0) * has_prev
        a_next = pltpu.roll(a, HALF - 1, 0) * has_next
        conv = a_prev * cw_ref[0:1, :] + a * cw_ref[1:2, :] + a_next * cw_ref[2:3, :] + cb_ref[...]
        return _dot((_silu(conv) * g).astype(BF16), wd_ref[...])

    def step(first, last):
        for s in range(TM // HALF):
            rows = slice(s * HALF, (s + 1) * HALF)
            hb = mixer_out(rows) if first else h_scr[rows, :]
            d = ffn_tile(hb)
            if not first:
                d = acc_scr[rows, :] + d
            if last:
                out = x_scr[rows, :] + _mod_part(mod_ref, 5) * d
                if is_last:
                    out = _rms(out, fg_ref[...])
                out_scr[rows, :] = out
            else:
                acc_scr[rows, :] = d

    pl.when(j == 0)(lambda: step(True, n_f == 1))
    if n_f > 2:
        pl.when((j > 0) & (j < n_f - 1))(lambda: step(False, False))
    if n_f > 1:
        pl.when(j == n_f - 1)(lambda: step(False, True))

    @pl.when(j == n_f - 1)
    def _():
        _write_rows(i, outc_ref, outl_ref, out_scr[...])


def _mix_ffn(is_gla, is_last, xc, xl, mix_in, w_o, mods, layer, norm_g, w_up, conv_w, conv_b,
             w_down, final_g):
    n_f = FFN_DIM // TF
    row = pl.BlockSpec((TM, D_MODEL), lambda i, j: (i, 0))
    if is_gla:
        mix_specs = [row, row, _layer_spec((1, GLA_DV), layer // 2)]
    else:
        mix_specs = [_ctx_rows(), _lat_rows()]
    in_specs = [_ctx_rows(), _lat_rows()] + mix_specs + [
        _layer_spec((D_MODEL, D_MODEL), layer // 2), _mod_spec(layer),
        _layer_spec((1, D_MODEL), layer),
        pl.BlockSpec((None, D_MODEL, 2 * TF), lambda i, j: (layer, 0, j)),
        pl.BlockSpec((None, 3, TF), lambda i, j: (layer, 0, j)),
        pl.BlockSpec((None, 1, TF), lambda i, j: (layer, 0, j)),
        pl.BlockSpec((None, TF, D_MODEL), lambda i, j: (layer, j, 0)),
        _const_spec((1, D_MODEL)),
    ]
    return pl.pallas_call(
        functools.partial(_mix_ffn_kernel, is_gla, is_last),
        grid=(N_TOK // TM, FFN_DIM // TF),
        in_specs=in_specs,
        out_specs=[_ctx_rows(), _lat_rows()],
        out_shape=[jax.ShapeDtypeStruct((N_CTX_TOK, D_MODEL), F32),
                   jax.ShapeDtypeStruct((N_LAT_TOK, D_MODEL), F32)],
        scratch_shapes=[pltpu.VMEM((TM, D_MODEL), F32), pltpu.VMEM((TM, D_MODEL), BF16),
                        pltpu.VMEM((TM, D_MODEL), F32), pltpu.VMEM((TM, D_MODEL), F32),
                        pltpu.VMEM((D_MODEL, D_MODEL), BF16)],
        compiler_params=_params(("arbitrary", "arbitrary")),
        name="gla_out_ffn" if is_gla else "fnet_out_ffn",
    )(xc, xl, *mix_in, w_o, mods, norm_g.reshape(DEPTH, 1, D_MODEL), w_up, conv_w,
      conv_b.reshape(DEPTH, 1, FFN_DIM), w_down, final_g)


def _dft_tables(n, scale):
    idx = np.arange(n, dtype=np.int64)
    ang = (2.0 * np.pi / n) * ((idx[:, None] * idx[None, :]) % n).astype(np.float64)
    return (np.cos(ang) * scale).astype(np.float32), (np.sin(ang) * scale).astype(np.float32)


def _chunk_triangles():
    idx = np.arange(SUPER)
    same = (idx[:, None] // CHUNK) == (idx[None, :] // CHUNK)
    low = same & (idx[None, :] <= idx[:, None])
    up = same & (idx[None, :] >= idx[:, None])
    return low.astype(np.float32), up.astype(np.float32)


def kernel(x_prompt, x_sample, state_gla, c, c_ctx, norm_mix_g, norm_ffn_g, w_mod, b_mod,
           gla_w_in, gla_w_g2, gla_b_g, gla_norm_g, gla_w_o, fnet_w_in, fnet_w_o,
           ffn_w_up, ffn_conv_w, ffn_conv_b, ffn_w_down, final_norm_g):
    xc = x_prompt.reshape(N_CTX_TOK, D_MODEL)
    xl = x_sample.reshape(N_LAT_TOK, D_MODEL)
    c_cols = jnp.concatenate([c_ctx[None, :], c], axis=0)[:, :, None]
    mods = _mods(c_cols, w_mod, b_mod)

    tlow_np, tup_np = _chunk_triangles()
    tlow, tup = jnp.asarray(tlow_np).astype(BF16), jnp.asarray(tup_np).astype(BF16)
    cc, sc = _dft_tables(FNET_GW, FNET_GW ** -0.5)
    cs_chan = jnp.asarray(np.concatenate([cc, sc], axis=1)).astype(BF16)
    c_ctx_tab, s_ctx_tab = (jnp.asarray(t).astype(BF16) for t in _dft_tables(SEQ, 1.0))
    c_lat_tab, s_lat_tab = (jnp.asarray(t).astype(BF16) for t in _dft_tables(HALF_SEQ, 1.0))
    phase = (2.0 * np.pi / DEC_SEQ) * np.arange(HALF_SEQ, dtype=np.float64)
    twiddle = jnp.asarray(np.stack([np.cos(phase), np.sin(phase)], axis=1).astype(np.float32))
    final_g = final_norm_g.reshape(1, D_MODEL)

    n_f = FFN_DIM // TF
    w_up = jnp.concatenate(
        [ffn_w_up[:, :, part * FFN_DIM + t * TF:part * FFN_DIM + (t + 1) * TF]
         for t in range(n_f) for part in range(2)], axis=-1).astype(BF16)
    w_down = ffn_w_down.astype(BF16)
    head_norm_g = gla_norm_g.reshape(N_GLA_LAYERS, 1, GLA_DV)

    states = None
    for layer in range(DEPTH):
        j = layer // 2
        if layer % 2 == 0:
            qf, kf, qb, kb, v, r, oi, df, db = _gla_in(
                xc, xl, mods, layer, norm_mix_g, gla_w_in, gla_w_g2, gla_b_g, tlow, tup)
            o, states = _gla_scan(qf, kf, qb, kb, v, oi, df, db, state_gla, j, states)
            mix_in = (o, r, head_norm_g)
            w_o = gla_w_o
        else:
            uc, us = _fnet_in(xc, xl, mods, layer, norm_mix_g, fnet_w_in, cs_chan,
                              c_ctx_tab, s_ctx_tab)
            f_lat = _fnet_dft_lat(uc, us, c_lat_tab, s_lat_tab, twiddle)
            mix_in = (uc, f_lat)
            w_o = fnet_w_o
        xc, xl = _mix_ffn(layer % 2 == 0, layer == DEPTH - 1, xc, xl, mix_in, w_o, mods, layer,
                          norm_ffn_g, w_up, ffn_conv_w, ffn_conv_b, w_down, final_g)

    return (xc.reshape(BATCH, SEQ, D_MODEL), xl.reshape(DEC_BATCH, DEC_SEQ, D_MODEL), states)
```

```python
import functools

import numpy as np
import jax
import jax.numpy as jnp
from jax import lax
from jax.experimental import pallas as pl
from jax.experimental.pallas import tpu as pltpu

F32 = jnp.float32
BF16 = jnp.bfloat16

D_MODEL = 1024
BATCH = 16
SEQ = 256
DEPTH = 4
DEC_BATCH = 2
DEC_SEQ = 2048
GRID_W = 64
N_GLA_LAYERS = 2
GLA_HEADS = 4
GLA_QK = D_MODEL // 2
GLA_V = D_MODEL
GLA_DK = GLA_QK // GLA_HEADS
GLA_DV = GLA_V // GLA_HEADS
GATE_RANK = 16
GATE_TAU = 16.0
GLA_IN = 2 * GLA_QK + 2 * GLA_V + 2 * GATE_RANK
FNET_GROUPS = 4
FNET_GW = D_MODEL // FNET_GROUPS
FFN_DIM = 2816
EPS = 1e-6

N_CTX_TOK = BATCH * SEQ
N_LAT_TOK = DEC_BATCH * DEC_SEQ
N_TOK = N_CTX_TOK + N_LAT_TOK
N_GROUPS = 1 + DEC_BATCH

LANES = 128
SUBLANES = 8
CHUNK = 64
SUPER = 256
TM = 512
N_CTX_BLK = N_CTX_TOK // TM
SCAN_ROWS = 2048
N_CTX_SCAN = N_CTX_TOK // SCAN_ROWS
SCAN_STEPS = 8
HALF = 256
TF = 1408
GLA_IN_PAD = 3328
GLR_COL = 2 * GLA_QK + 2 * GLA_V
VMEM_LIMIT = 56 * 1024 * 1024
MOD_ROWS = 512
MOD_TILES = 8


def _dot(a, b):
    return jnp.dot(a, b, preferred_element_type=F32)


def _dot_nt(a, b):
    return lax.dot_general(a, b, (((1,), (1,)), ((), ())), preferred_element_type=F32)


def _dot_tn(a, b):
    return lax.dot_general(a, b, (((0,), (0,)), ((), ())), preferred_element_type=F32)


def _silu(x):
    return x / (1.0 + jnp.exp(-x))


def _log_sigmoid(x):
    return jnp.minimum(x, 0.0) - jnp.log(1.0 + jnp.exp(-jnp.abs(x)))


def _rms(x, g):
    return x * lax.rsqrt(jnp.mean(x * x, axis=-1, keepdims=True) + EPS) * g


def _group_of_block(i):
    return jnp.where(i < N_CTX_BLK, 0, 1 + (i - N_CTX_BLK) // (DEC_SEQ // TM))


def _params(sem):
    return pltpu.CompilerParams(dimension_semantics=sem, vmem_limit_bytes=VMEM_LIMIT)


def _const_spec(shape):
    nd = len(shape)
    return pl.BlockSpec(shape, lambda *_: (0,) * nd, pipeline_mode=pl.Buffered(1))


def _layer_spec(shape, layer):
    nd = len(shape)
    return pl.BlockSpec((None,) + tuple(shape), lambda *_: (layer,) + (0,) * nd,
                        pipeline_mode=pl.Buffered(1))


def _mod_spec(layer):
    return pl.BlockSpec((None, None, 1, 6 * D_MODEL),
                        lambda i, *_: (layer, _group_of_block(i), 0, 0))


def _mod_part(mod_ref, k):
    return mod_ref[:, k * D_MODEL:(k + 1) * D_MODEL]


def _ctx_rows(width=D_MODEL):
    return pl.BlockSpec((TM, width), lambda i, *_: (jnp.minimum(i, N_CTX_BLK - 1), 0))


def _lat_rows(width=D_MODEL):
    return pl.BlockSpec((TM, width), lambda i, *_: (jnp.maximum(i - N_CTX_BLK, 0), 0))


def _read_rows(i, ctx_ref, lat_ref, rows=slice(None)):
    return jnp.where(i < N_CTX_BLK, ctx_ref[rows, :], lat_ref[rows, :])


def _write_rows(i, ctx_ref, lat_ref, value):
    @pl.when(i < N_CTX_BLK)
    def _():
        ctx_ref[...] = value

    @pl.when(i >= N_CTX_BLK)
    def _():
        lat_ref[...] = value


def _mods_kernel(c_ref, w_ref, b_ref, o_ref, sb_scr):
    k = pl.program_id(1)

    @pl.when((pl.program_id(0) == 0) & (k == 0))
    def _():
        for v in range(N_GROUPS):
            sb_scr[v] = jnp.broadcast_to(_silu(c_ref[v]), (D_MODEL, LANES))

    @pl.when(k == 0)
    def _():
        for v in range(N_GROUPS):
            o_ref[v] = b_ref[...]

    row_base = k * MOD_ROWS
    zero = jnp.zeros((SUBLANES, LANES), F32)
    for grp in range(6 * D_MODEL // (MOD_TILES * LANES)):
        col0 = grp * MOD_TILES * LANES

        def slab(r, accs, col0=col0):
            r0 = pl.multiple_of(r * SUBLANES, SUBLANES)
            s = [sb_scr[v, pl.ds(row_base + r0, SUBLANES), :] for v in range(N_GROUPS)]
            out = []
            for t in range(MOD_TILES):
                w = w_ref[pl.ds(r0, SUBLANES), col0 + t * LANES:col0 + (t + 1) * LANES]
                out += [accs[t * N_GROUPS + v] + w * s[v] for v in range(N_GROUPS)]
            return tuple(out)

        accs = lax.fori_loop(0, MOD_ROWS // SUBLANES, slab, (zero,) * (MOD_TILES * N_GROUPS),
                             unroll=8)
        for t in range(MOD_TILES):
            cols = slice(col0 + t * LANES, col0 + (t + 1) * LANES)
            for v in range(N_GROUPS):
                o_ref[v, :, cols] += jnp.sum(accs[t * N_GROUPS + v], axis=0, keepdims=True)


def _mods(c_cols, w_mod, b_mod):
    n_out = 6 * D_MODEL
    return pl.pallas_call(
        _mods_kernel,
        grid=(DEPTH, D_MODEL // MOD_ROWS),
        in_specs=[
            pl.BlockSpec((N_GROUPS, D_MODEL, 1), lambda l, k: (0, 0, 0)),
            pl.BlockSpec((None, MOD_ROWS, n_out), lambda l, k: (l, k, 0)),
            pl.BlockSpec((None, 1, n_out), lambda l, k: (l, 0, 0)),
        ],
        out_specs=pl.BlockSpec((None, N_GROUPS, 1, n_out), lambda l, k: (l, 0, 0, 0)),
        out_shape=jax.ShapeDtypeStruct((DEPTH, N_GROUPS, 1, n_out), F32),
        scratch_shapes=[pltpu.VMEM((N_GROUPS, D_MODEL, LANES), F32)],
        compiler_params=_params(("arbitrary", "arbitrary")),
        name="adaln_mods",
    )(c_cols, w_mod, b_mod.reshape(DEPTH, 1, n_out))


def _cumsum2(t, g):
    g1 = g.astype(BF16)
    g2 = (g - g1.astype(F32)).astype(BF16)
    return _dot(t, g1) + _dot(t, g2)


def _rows_bcast(b, first_row):
    n = b.shape[0] // CHUNK
    return jnp.concatenate(
        [jnp.broadcast_to(b[c * CHUNK + first_row:c * CHUNK + first_row + 1, :], (CHUNK, b.shape[1]))
         for c in range(n)], axis=0)


def _gla_in_kernel(xc_ref, xl_ref, mod_ref, ng_ref, w32_ref, wg32_ref, bg_ref, tlow_ref, tup_ref,
                   qf_ref, kf_ref, qb_ref, kb_ref, v_ref, r_ref, oi_ref, df_ref, db_ref,
                   w_ref, wg2_ref):
    i = pl.program_id(0)

    @pl.when(i == 0)
    def _():
        w_ref[:, 0:GLR_COL] = w32_ref[:, 0:GLR_COL].astype(BF16)
        w_ref[:, GLR_COL:GLA_IN_PAD] = jnp.zeros((D_MODEL, GLA_IN_PAD - GLR_COL), BF16)
        w_ref[:, GLR_COL:GLA_IN] = w32_ref[:, GLR_COL:GLA_IN].astype(BF16)
        wg2_ref[...] = jnp.zeros_like(wg2_ref)
        for z in range(2):
            wg2_ref[z, z * GATE_RANK:(z + 1) * GATE_RANK, :] = wg32_ref[z].astype(BF16)

    ri = lax.broadcasted_iota(jnp.int32, (SUPER, SUPER), 0)
    ci = lax.broadcasted_iota(jnp.int32, (SUPER, SUPER), 1)
    same = (ri & -CHUNK) == (ci & -CHUNK)
    low = same & (ci <= ri)
    up = same & (ci >= ri)
    cpg = SUPER // CHUNK
    inv_tau = 1.0 / GATE_TAU

    def project(s, res):
        rows = slice(s * SUPER, (s + 1) * SUPER)
        x = _read_rows(i, xc_ref, xl_ref, rows)
        h = _rms(x, ng_ref[...]) * (1.0 + _mod_part(mod_ref, 1)) + _mod_part(mod_ref, 0)
        hb = h.astype(BF16)
        yield
        qk = _dot(hb, w_ref[:, 0:2 * GLA_QK])
        res["q"] = qk[:, 0:GLA_QK] * (GLA_DK ** -0.5)
        res["k"] = qk[:, GLA_QK:2 * GLA_QK]
        yield
        res["v"] = _dot(hb, w_ref[:, 2 * GLA_QK:2 * GLA_QK + GLA_V]).astype(BF16)
        v_ref[rows, :] = res["v"]
        yield
        r_ref[rows, :] = _dot(hb, w_ref[:, 2 * GLA_QK + GLA_V:GLR_COL])
        yield
        res["glr"] = _dot(hb, w_ref[:, GLR_COL:GLA_IN_PAD]).astype(BF16)

    def local(s, res):
        rows = slice(s * SUPER, (s + 1) * SUPER)
        qs, ks, vb, glr = res["q"], res["k"], res["v"], res["glr"]
        gf = _log_sigmoid(_dot(glr, wg2_ref[0]) + bg_ref[0:1, :]) * inv_tau
        gb = _log_sigmoid(_dot(glr, wg2_ref[1]) + bg_ref[1:2, :]) * inv_tau
        yield
        b = _cumsum2(tlow_ref[...], gf)
        bmid = _rows_bcast(b, CHUNK // 2 - 1)
        blast = _rows_bcast(b, CHUNK - 1)
        qtf = (qs * jnp.exp(b - bmid)).astype(BF16)
        ktf = (ks * jnp.exp(bmid - b)).astype(BF16)
        qf_ref[rows, :] = (qs * jnp.exp(b)).astype(BF16)
        kf_ref[rows, :] = (ks * jnp.exp(blast - b)).astype(BF16)
        for c in range(cpg):
            df_ref[s * cpg + c:s * cpg + c + 1, :] = jnp.exp(b[c * CHUNK + CHUNK - 1:(c + 1) * CHUNK, :])
        yield
        b = _cumsum2(tup_ref[...], gb)
        bmid = _rows_bcast(b, CHUNK // 2)
        blast = _rows_bcast(b, 0)
        qtb = (qs * jnp.exp(b - bmid)).astype(BF16)
        ktb = (ks * jnp.exp(bmid - b)).astype(BF16)
        qb_ref[rows, :] = (qs * jnp.exp(b)).astype(BF16)
        kb_ref[rows, :] = (ks * jnp.exp(blast - b)).astype(BF16)
        for c in range(cpg):
            db_ref[s * cpg + c:s * cpg + c + 1, :] = jnp.exp(b[c * CHUNK:c * CHUNK + 1, :])
        yield
        for hd in range(GLA_HEADS):
            kc = slice(hd * GLA_DK, (hd + 1) * GLA_DK)
            af = _dot_nt(qtf[:, kc], ktf[:, kc])
            ab = _dot_nt(qtb[:, kc], ktb[:, kc])
            a = (jnp.where(low, af, 0.0) + jnp.where(up, ab, 0.0)).astype(BF16)
            vc = slice(hd * GLA_DV, (hd + 1) * GLA_DV)
            oi_ref[rows, vc] = _dot(a, vb[:, vc])

    n_groups = TM // SUPER
    results = [dict() for _ in range(n_groups)]
    for s in range(n_groups + 1):
        stages = []
        if s < n_groups:
            stages.append(project(s, results[s]))
        if s > 0:
            stages.append(local(s - 1, results[s - 1]))
        while stages:
            stages = [g for g in stages if next(g, "done") != "done"]


def _gla_in(xc, xl, mods, layer, norm_g, w_in, w_g2, b_g, tlow, tup):
    gla_layer = layer // 2
    cpb = TM // CHUNK
    row = lambda w: pl.BlockSpec((TM, w), lambda i: (i, 0))
    outs = [
        jax.ShapeDtypeStruct((N_TOK, GLA_QK), BF16),
        jax.ShapeDtypeStruct((N_TOK, GLA_QK), BF16),
        jax.ShapeDtypeStruct((N_TOK, GLA_QK), BF16),
        jax.ShapeDtypeStruct((N_TOK, GLA_QK), BF16),
        jax.ShapeDtypeStruct((N_TOK, GLA_V), BF16),
        jax.ShapeDtypeStruct((N_TOK, GLA_V), F32),
        jax.ShapeDtypeStruct((N_TOK, GLA_V), F32),
        jax.ShapeDtypeStruct((N_TOK // CHUNK, GLA_QK), F32),
        jax.ShapeDtypeStruct((N_TOK // CHUNK, GLA_QK), F32),
    ]
    return pl.pallas_call(
        _gla_in_kernel,
        grid=(N_TOK // TM,),
        in_specs=[
            _ctx_rows(), _lat_rows(), _mod_spec(layer), _layer_spec((1, D_MODEL), layer),
            _layer_spec((D_MODEL, GLA_IN), gla_layer),
            _layer_spec((2, GATE_RANK, GLA_QK), gla_layer), _layer_spec((2, GLA_QK), gla_layer),
            _const_spec((SUPER, SUPER)), _const_spec((SUPER, SUPER)),
        ],
        out_specs=[row(GLA_QK)] * 4 + [row(GLA_V)] * 3
        + [pl.BlockSpec((cpb, GLA_QK), lambda i: (i, 0))] * 2,
        out_shape=outs,
        scratch_shapes=[pltpu.VMEM((D_MODEL, GLA_IN_PAD), BF16),
                        pltpu.VMEM((2, GLA_IN_PAD - GLR_COL, GLA_QK), BF16)],
        compiler_params=_params(("arbitrary",)),
        name="gla_in",
    )(xc, xl, mods, norm_g.reshape(DEPTH, 1, D_MODEL), w_in, w_g2, b_g, tlow, tup)


def _scan_steps(q_ref, k_ref, d_ref, v_ref, chunks, st):
    rows = [pl.multiple_of(ch * CHUNK, CHUNK) for ch in chunks]
    updates = [_dot_tn(v_ref[pl.ds(r0, CHUNK), :], k_ref[pl.ds(r0, CHUNK), :]) for r0 in rows]
    reads = []
    for ch, r0, u in zip(chunks, rows, updates):
        reads.append((r0, _dot_nt(q_ref[pl.ds(r0, CHUNK), :], st.astype(BF16))))
        st = st * d_ref[pl.ds(ch, 1), :] + u
    return st, reads


def _gla_scan_kernel(has_prev, *refs):
    (qf_ref, kf_ref, qb_ref, kb_ref, v_ref, oi_ref, df_ref, db_ref, s0_ref) = refs[:9]
    refs = refs[9:]
    if has_prev:
        prev_ref, refs = refs[0], refs[1:]
    o_ref, st_ref, s_scr = refs
    i = pl.program_id(1)
    o_ref[...] = oi_ref[...]

    def accumulate(reads):
        for r0, val in reads:
            o_ref[pl.ds(r0, CHUNK), :] += val

    @pl.when(i < N_CTX_SCAN)
    def _():
        cps = SEQ // CHUNK
        seq_per_trip = SCAN_STEPS // cps
        layer_slot = N_GLA_LAYERS - 1 if has_prev else 0
        if has_prev:
            for l in range(N_GLA_LAYERS - 1):
                st_ref[:, l] = prev_ref[:, l]

        def trip(t, carry):
            for u in range(seq_per_trip):
                s = t * seq_per_trip + u
                zero = jnp.zeros((GLA_DV, GLA_DK), F32)
                sf, rf = _scan_steps(qf_ref, kf_ref, df_ref, v_ref,
                                     [s * cps + c for c in range(cps)], zero)
                sb, rb = _scan_steps(qb_ref, kb_ref, db_ref, v_ref,
                                     [s * cps + (cps - 1 - c) for c in range(cps)], zero)
                accumulate(rf + rb)
                st_ref[s, layer_slot, 0] = sf.T
                st_ref[s, layer_slot, 1] = sb.T
            return carry

        lax.fori_loop(0, SCAN_ROWS // SEQ // seq_per_trip, trip, 0)

    @pl.when(i >= N_CTX_SCAN)
    def _():
        n_chunks = SCAN_ROWS // CHUNK
        s_scr[0] = s0_ref[0].T
        s_scr[1] = s0_ref[1].T

        def trip(t, carry):
            first = t * SCAN_STEPS
            sf, rf = _scan_steps(qf_ref, kf_ref, df_ref, v_ref,
                                 [first + c for c in range(SCAN_STEPS)], s_scr[0])
            sb, rb = _scan_steps(qb_ref, kb_ref, db_ref, v_ref,
                                 [n_chunks - 1 - first - c for c in range(SCAN_STEPS)], s_scr[1])
            accumulate(rf + rb)
            s_scr[0] = sf
            s_scr[1] = sb
            return carry

        lax.fori_loop(0, n_chunks // SCAN_STEPS, trip, 0)


def _gla_scan(qf, kf, qb, kb, v, oi, df, db, state_gla, gla_layer, prev_states):
    has_prev = prev_states is not None
    n_layers_out = gla_layer + 1
    cpb = SCAN_ROWS // CHUNK
    seq_per_blk = SCAN_ROWS // SEQ
    qk_spec = pl.BlockSpec((SCAN_ROWS, GLA_DK), lambda h, i: (i, h))
    v_spec = pl.BlockSpec((SCAN_ROWS, GLA_DV), lambda h, i: (i, h))
    d_spec = pl.BlockSpec((cpb, GLA_DK), lambda h, i: (i, h))
    ctx_blk = lambda h, i: jnp.minimum(i, N_CTX_SCAN - 1)
    in_specs = [qk_spec, qk_spec, qk_spec, qk_spec, v_spec, v_spec, d_spec, d_spec,
                pl.BlockSpec((None, None, 2, None, GLA_DK, GLA_DV),
                             lambda h, i: (jnp.maximum(i - N_CTX_SCAN, 0), gla_layer, 0, h, 0, 0))]
    args = [qf, kf, qb, kb, v, oi, df, db, state_gla]
    if has_prev:
        in_specs.append(pl.BlockSpec((seq_per_blk, gla_layer, 2, None, GLA_DK, GLA_DV),
                                     lambda h, i: (ctx_blk(h, i), 0, 0, h, 0, 0)))
        args.append(prev_states)
    return pl.pallas_call(
        functools.partial(_gla_scan_kernel, has_prev),
        grid=(GLA_HEADS, N_TOK // SCAN_ROWS),
        in_specs=in_specs,
        out_specs=[
            v_spec,
            pl.BlockSpec((seq_per_blk, n_layers_out, 2, None, GLA_DK, GLA_DV),
                         lambda h, i: (ctx_blk(h, i), 0, 0, h, 0, 0)),
        ],
        out_shape=[jax.ShapeDtypeStruct((N_TOK, GLA_V), F32),
                   jax.ShapeDtypeStruct((BATCH, n_layers_out, 2, GLA_HEADS, GLA_DK, GLA_DV), F32)],
        scratch_shapes=[pltpu.VMEM((2, GLA_DV, GLA_DK), F32)],
        compiler_params=_params(("arbitrary", "arbitrary")),
        name="gla_scan",
    )(*args)


def _fnet_in_kernel(xc_ref, xl_ref, mod_ref, ng_ref, w32_ref, cs_ref, c_ref, s_ref, uc_ref, us_ref,
                    w_ref):
    i = pl.program_id(0)

    @pl.when(i == 0)
    def _():
        w_ref[...] = w32_ref[...].astype(BF16)

    x = _read_rows(i, xc_ref, xl_ref)
    h = _rms(x, ng_ref[...]) * (1.0 + _mod_part(mod_ref, 1)) + _mod_part(mod_ref, 0)
    u = _dot(h.astype(BF16), w_ref[...]).astype(BF16)
    for g in range(FNET_GROUPS):
        cols = slice(g * FNET_GW, (g + 1) * FNET_GW)
        t = _dot(u[:, cols], cs_ref[...])
        uc_ref[:, cols] = t[:, 0:FNET_GW].astype(BF16)
        us_ref[:, cols] = t[:, FNET_GW:2 * FNET_GW].astype(BF16)

    @pl.when(i < N_CTX_BLK)
    def _():
        for s in range(TM // SEQ):
            rows = slice(s * SEQ, (s + 1) * SEQ)
            f = _dot(c_ref[...], uc_ref[rows, :]) - _dot(s_ref[...], us_ref[rows, :])
            uc_ref[rows, :] = (f * SEQ ** -0.5).astype(BF16)


def _fnet_in(xc, xl, mods, layer, norm_g, w_in, cs_chan, c_tab, s_tab):
    row = pl.BlockSpec((TM, D_MODEL), lambda i: (i, 0))
    return pl.pallas_call(
        _fnet_in_kernel,
        grid=(N_TOK // TM,),
        in_specs=[_ctx_rows(), _lat_rows(), _mod_spec(layer), _layer_spec((1, D_MODEL), layer),
                  _layer_spec((D_MODEL, D_MODEL), layer // 2), _const_spec((FNET_GW, 2 * FNET_GW)),
                  _const_spec((SEQ, SEQ)), _const_spec((SEQ, SEQ))],
        out_specs=[row, row],
        out_shape=[jax.ShapeDtypeStruct((N_TOK, D_MODEL), BF16)] * 2,
        scratch_shapes=[pltpu.VMEM((D_MODEL, D_MODEL), BF16)],
        compiler_params=_params(("arbitrary",)),
        name="fnet_in",
    )(xc, xl, mods, norm_g.reshape(DEPTH, 1, D_MODEL), w_in, cs_chan, c_tab, s_tab)


HALF_SEQ = DEC_SEQ // 2


def _dft_lat_kernel(uc_ref, us_ref, c_ref, s_ref, tw_ref, f_ref, ec_scr, es_scr, oc_scr, os_scr):
    @pl.when(pl.program_id(1) == 0)
    def _():
        lo, hi = slice(0, HALF_SEQ), slice(HALF_SEQ, DEC_SEQ)
        c_lo, c_hi = uc_ref[lo, :].astype(F32), uc_ref[hi, :].astype(F32)
        s_lo, s_hi = us_ref[lo, :].astype(F32), us_ref[hi, :].astype(F32)
        ec_scr[...] = (c_lo + c_hi).astype(BF16)
        es_scr[...] = (s_lo + s_hi).astype(BF16)
        dc, ds = c_lo - c_hi, s_lo - s_hi
        cos_l, sin_l = tw_ref[:, 0:1], tw_ref[:, 1:2]
        oc_scr[...] = (cos_l * dc - sin_l * ds).astype(BF16)
        os_scr[...] = (sin_l * dc + cos_l * ds).astype(BF16)

    scale = DEC_SEQ ** -0.5
    even = (_dot(c_ref[...], ec_scr[...]) - _dot(s_ref[...], es_scr[...])) * scale
    odd = (_dot(c_ref[...], oc_scr[...]) - _dot(s_ref[...], os_scr[...])) * scale
    for t in range(D_MODEL // LANES):
        cols = slice(t * LANES, (t + 1) * LANES)
        f_ref[t, pl.ds(0, TM, stride=2), :] = even[:, cols]
        f_ref[t, pl.ds(1, TM, stride=2), :] = odd[:, cols]


def _fnet_dft_lat(uc, us, c_half, s_half, twiddle):
    n_ctx_blk = N_CTX_TOK // DEC_SEQ
    m_tiles = HALF_SEQ // TM
    seq_blk = pl.BlockSpec((DEC_SEQ, D_MODEL), lambda b, m: (n_ctx_blk + b, 0))
    tab = pl.BlockSpec((TM, HALF_SEQ), lambda b, m: (m, 0))
    return pl.pallas_call(
        _dft_lat_kernel,
        grid=(DEC_BATCH, m_tiles),
        in_specs=[seq_blk, seq_blk, tab, tab, _const_spec((HALF_SEQ, 2))],
        out_specs=pl.BlockSpec((D_MODEL // LANES, 2 * TM, LANES),
                               lambda b, m: (0, b * m_tiles + m, 0)),
        out_shape=jax.ShapeDtypeStruct((D_MODEL // LANES, N_LAT_TOK, LANES), F32),
        scratch_shapes=[pltpu.VMEM((HALF_SEQ, D_MODEL), BF16)] * 4,
        compiler_params=_params(("arbitrary", "arbitrary")),
        name="fnet_dft_lat",
    )(uc, us, c_half, s_half, twiddle)


def _mix_ffn_kernel(is_gla, is_last, *refs):
    xc_ref, xl_ref = refs[:2]
    if is_gla:
        o_ref, r_ref, gn_ref = refs[2:5]
        refs = refs[5:]
    else:
        fc_ref, fl_ref = refs[2:4]
        refs = refs[4:]
    (wo32_ref, mod_ref, ng_ref, wa_ref, wg_ref, cw_ref, cb_ref, wd_ref, fg_ref,
     outc_ref, outl_ref, x_scr, h_scr, acc_scr, out_scr, wo_ref) = refs
    i = pl.program_id(0)
    j = pl.program_id(1)
    n_f = FFN_DIM // TF

    @pl.when((i == 0) & (j == 0))
    def _():
        wo_ref[...] = wo32_ref[...].astype(BF16)

    period = jnp.where(i < N_CTX_BLK, SEQ, GRID_W)
    pos = lax.broadcasted_iota(jnp.int32, (HALF, 1), 0) & (period - 1)
    has_prev = (pos != 0).astype(F32)
    has_next = (pos != period - 1).astype(F32)

    def mixer_out(rows):
        if is_gla:
            parts = []
            for hd in range(GLA_HEADS):
                cols = slice(hd * GLA_DV, (hd + 1) * GLA_DV)
                y = _rms(o_ref[rows, cols], gn_ref[...]) * _silu(r_ref[rows, cols])
                parts.append(y.astype(BF16))
            y = jnp.concatenate(parts, axis=1)
        else:
            f_lat = jnp.concatenate([fl_ref[t, rows, :] for t in range(D_MODEL // LANES)], axis=1)
            y = jnp.where(i < N_CTX_BLK, fc_ref[rows, :], f_lat.astype(BF16))
        x_new = _read_rows(i, xc_ref, xl_ref, rows) + _mod_part(mod_ref, 2) * _dot(y, wo_ref[...])
        x_scr[rows, :] = x_new
        h = _rms(x_new, ng_ref[...]) * (1.0 + _mod_part(mod_ref, 4)) + _mod_part(mod_ref, 3)
        hb = h.astype(BF16)
        h_scr[rows, :] = hb
        return hb

    def ffn_tile(hb):
        a = _dot(hb, wa_ref[...])
        g = _dot(hb, wg_ref[...])
        a_prev = pltpu.roll(a, 1, 0) * has_prev
        a_next = pltpu.roll(a, HALF - 1, 0) * has_next
        conv = a_prev * cw_ref[0:1, :] + a * cw_ref[1:2, :] + a_next * cw_ref[2:3, :] + cb_ref[...]
        return _dot((_silu(conv) * g).astype(BF16), wd_ref[...])

    def step(first, last):
        for s in range(TM // HALF):
            rows = slice(s * HALF, (s + 1) * HALF)
            hb = mixer_out(rows) if first else h_scr[rows, :]
            d = ffn_tile(hb)
            if not first:
                d = acc_scr[rows, :] + d
            if last:
                out = x_scr[rows, :] + _mod_part(mod_ref, 5) * d
                if is_last:
                    out = _rms(out, fg_ref[...])
                out_scr[rows, :] = out
            else:
                acc_scr[rows, :] = d

    pl.when(j == 0)(lambda: step(True, n_f == 1))
    if n_f > 2:
        pl.when((j > 0) & (j < n_f - 1))(lambda: step(False, False))
    if n_f > 1:
        pl.when(j == n_f - 1)(lambda: step(False, True))

    @pl.when(j == n_f - 1)
    def _():
        _write_rows(i, outc_ref, outl_ref, out_scr[...])


def _mix_ffn(is_gla, is_last, xc, xl, mix_in, w_o, mods, layer, norm_g, w_up, conv_w, conv_b,
             w_down, final_g):
    n_f = FFN_DIM // TF
    row = pl.BlockSpec((TM, D_MODEL), lambda i, j: (i, 0))
    if is_gla:
        mix_specs = [row, row, _layer_spec((1, GLA_DV), layer // 2)]
    else:
        mix_specs = [_ctx_rows(),
                     pl.BlockSpec((D_MODEL // LANES, TM, LANES),
                                  lambda i, j: (0, jnp.maximum(i - N_CTX_BLK, 0), 0))]
    in_specs = [_ctx_rows(), _lat_rows()] + mix_specs + [
        _layer_spec((D_MODEL, D_MODEL), layer // 2), _mod_spec(layer),
        _layer_spec((1, D_MODEL), layer),
        pl.BlockSpec((None, D_MODEL, TF), lambda i, j: (layer, 0, j)),
        pl.BlockSpec((None, D_MODEL, TF), lambda i, j: (layer, 0, n_f + j)),
        pl.BlockSpec((None, 3, TF), lambda i, j: (layer, 0, j)),
        pl.BlockSpec((None, 1, TF), lambda i, j: (layer, 0, j)),
        pl.BlockSpec((None, TF, D_MODEL), lambda i, j: (layer, j, 0)),
        _const_spec((1, D_MODEL)),
    ]
    return pl.pallas_call(
        functools.partial(_mix_ffn_kernel, is_gla, is_last),
        grid=(N_TOK // TM, FFN_DIM // TF),
        in_specs=in_specs,
        out_specs=[_ctx_rows(), _lat_rows()],
        out_shape=[jax.ShapeDtypeStruct((N_CTX_TOK, D_MODEL), F32),
                   jax.ShapeDtypeStruct((N_LAT_TOK, D_MODEL), F32)],
        scratch_shapes=[pltpu.VMEM((TM, D_MODEL), F32), pltpu.VMEM((TM, D_MODEL), BF16),
                        pltpu.VMEM((TM, D_MODEL), F32), pltpu.VMEM((TM, D_MODEL), F32),
                        pltpu.VMEM((D_MODEL, D_MODEL), BF16)],
        compiler_params=_params(("arbitrary", "arbitrary")),
        name="gla_out_ffn" if is_gla else "fnet_out_ffn",
    )(xc, xl, *mix_in, w_o, mods, norm_g.reshape(DEPTH, 1, D_MODEL), w_up, w_up, conv_w,
      conv_b.reshape(DEPTH, 1, FFN_DIM), w_down, final_g)


def _dft_tables(n, scale):
    idx = np.arange(n, dtype=np.int64)
    ang = (2.0 * np.pi / n) * ((idx[:, None] * idx[None, :]) % n).astype(np.float64)
    return (np.cos(ang) * scale).astype(np.float32), (np.sin(ang) * scale).astype(np.float32)


def _chunk_triangles():
    idx = np.arange(SUPER)
    same = (idx[:, None] // CHUNK) == (idx[None, :] // CHUNK)
    low = same & (idx[None, :] <= idx[:, None])
    up = same & (idx[None, :] >= idx[:, None])
    return low.astype(np.float32), up.astype(np.float32)


def kernel(x_prompt, x_sample, state_gla, c, c_ctx, norm_mix_g, norm_ffn_g, w_mod, b_mod,
           gla_w_in, gla_w_g2, gla_b_g, gla_norm_g, gla_w_o, fnet_w_in, fnet_w_o,
           ffn_w_up, ffn_conv_w, ffn_conv_b, ffn_w_down, final_norm_g):
    xc = x_prompt.reshape(N_CTX_TOK, D_MODEL)
    xl = x_sample.reshape(N_LAT_TOK, D_MODEL)
    c_cols = jnp.concatenate([c_ctx[None, :], c], axis=0)[:, :, None]
    mods = _mods(c_cols, w_mod, b_mod)

    tlow_np, tup_np = _chunk_triangles()
    tlow, tup = jnp.asarray(tlow_np).astype(BF16), jnp.asarray(tup_np).astype(BF16)
    cc, sc = _dft_tables(FNET_GW, FNET_GW ** -0.5)
    cs_chan = jnp.asarray(np.concatenate([cc, sc], axis=1)).astype(BF16)
    c_ctx_tab, s_ctx_tab = (jnp.asarray(t).astype(BF16) for t in _dft_tables(SEQ, 1.0))
    c_lat_tab, s_lat_tab = (jnp.asarray(t).astype(BF16) for t in _dft_tables(HALF_SEQ, 1.0))
    phase = (2.0 * np.pi / DEC_SEQ) * np.arange(HALF_SEQ, dtype=np.float64)
    twiddle = jnp.asarray(np.stack([np.cos(phase), np.sin(phase)], axis=1).astype(np.float32))
    final_g = final_norm_g.reshape(1, D_MODEL)

    w_up = ffn_w_up.astype(BF16)
    w_down = ffn_w_down.astype(BF16)
    head_norm_g = gla_norm_g.reshape(N_GLA_LAYERS, 1, GLA_DV)

    states = None
    for layer in range(DEPTH):
        j = layer // 2
        if layer % 2 == 0:
            qf, kf, qb, kb, v, r, oi, df, db = _gla_in(
                xc, xl, mods, layer, norm_mix_g, gla_w_in, gla_w_g2, gla_b_g, tlow, tup)
            o, states = _gla_scan(qf, kf, qb, kb, v, oi, df, db, state_gla, j, states)
            mix_in = (o, r, head_norm_g)
            w_o = gla_w_o
        else:
            uc, us = _fnet_in(xc, xl, mods, layer, norm_mix_g, fnet_w_in, cs_chan,
                              c_ctx_tab, s_ctx_tab)
            f_lat = _fnet_dft_lat(uc, us, c_lat_tab, s_lat_tab, twiddle)
            mix_in = (uc, f_lat)
            w_o = fnet_w_o
        xc, xl = _mix_ffn(layer % 2 == 0, layer == DEPTH - 1, xc, xl, mix_in, w_o, mods, layer,
                          norm_ffn_g, w_up, ffn_conv_w, ffn_conv_b, w_down, final_g)

    return (xc.reshape(BATCH, SEQ, D_MODEL), xl.reshape(DEC_BATCH, DEC_SEQ, D_MODEL), states)
```

```python
import functools

import numpy as np
import jax
import jax.numpy as jnp
from jax import lax
from jax.experimental import pallas as pl
from jax.experimental.pallas import tpu as pltpu

F32 = jnp.float32
BF16 = jnp.bfloat16

D_MODEL = 1024
BATCH = 16
SEQ = 256
DEPTH = 4
DEC_BATCH = 2
DEC_SEQ = 2048
GRID_W = 64
N_GLA_LAYERS = 2
GLA_HEADS = 4
GLA_QK = D_MODEL // 2
GLA_V = D_MODEL
GLA_DK = GLA_QK // GLA_HEADS
GLA_DV = GLA_V // GLA_HEADS
GATE_RANK = 16
GATE_TAU = 16.0
GLA_IN = 2 * GLA_QK + 2 * GLA_V + 2 * GATE_RANK
FNET_GROUPS = 4
FNET_GW = D_MODEL // FNET_GROUPS
FFN_DIM = 2816
EPS = 1e-6

N_CTX_TOK = BATCH * SEQ
N_LAT_TOK = DEC_BATCH * DEC_SEQ
N_TOK = N_CTX_TOK + N_LAT_TOK
N_GROUPS = 1 + DEC_BATCH

LANES = 128
SUBLANES = 8
CHUNK = 64
SUPER = 256
TM = 512
N_CTX_BLK = N_CTX_TOK // TM
SCAN_ROWS = 2048
N_CTX_SCAN = N_CTX_TOK // SCAN_ROWS
SCAN_STEPS = 16
HALF = 256
TF = 1408
GLA_IN_PAD = 3328
GLR_COL = 2 * GLA_QK + 2 * GLA_V
VMEM_LIMIT = 56 * 1024 * 1024
MOD_ROWS = 512
MOD_TILES = 8


def _dot(a, b):
    return jnp.dot(a, b, preferred_element_type=F32)


def _dot_nt(a, b):
    return lax.dot_general(a, b, (((1,), (1,)), ((), ())), preferred_element_type=F32)


def _dot_tn(a, b):
    return lax.dot_general(a, b, (((0,), (0,)), ((), ())), preferred_element_type=F32)


def _silu(x):
    return x / (1.0 + jnp.exp(-x))


def _log_sigmoid(x):
    return jnp.minimum(x, 0.0) - jnp.log(1.0 + jnp.exp(-jnp.abs(x)))


def _rms(x, g):
    return x * lax.rsqrt(jnp.mean(x * x, axis=-1, keepdims=True) + EPS) * g


def _group_of_block(i):
    return jnp.where(i < N_CTX_BLK, 0, 1 + (i - N_CTX_BLK) // (DEC_SEQ // TM))


def _params(sem):
    return pltpu.CompilerParams(dimension_semantics=sem, vmem_limit_bytes=VMEM_LIMIT)


def _const_spec(shape):
    nd = len(shape)
    return pl.BlockSpec(shape, lambda *_: (0,) * nd, pipeline_mode=pl.Buffered(1))


def _layer_spec(shape, layer):
    nd = len(shape)
    return pl.BlockSpec((None,) + tuple(shape), lambda *_: (layer,) + (0,) * nd,
                        pipeline_mode=pl.Buffered(1))


def _mod_spec(layer):
    return pl.BlockSpec((None, None, 1, 6 * D_MODEL),
                        lambda i, *_: (layer, _group_of_block(i), 0, 0))


def _mod_part(mod_ref, k):
    return mod_ref[:, k * D_MODEL:(k + 1) * D_MODEL]


def _ctx_rows(width=D_MODEL):
    return pl.BlockSpec((TM, width), lambda i, *_: (jnp.minimum(i, N_CTX_BLK - 1), 0))


def _lat_rows(width=D_MODEL):
    return pl.BlockSpec((TM, width), lambda i, *_: (jnp.maximum(i - N_CTX_BLK, 0), 0))


def _read_rows(i, ctx_ref, lat_ref, rows=slice(None)):
    return jnp.where(i < N_CTX_BLK, ctx_ref[rows, :], lat_ref[rows, :])


def _write_rows(i, ctx_ref, lat_ref, value):
    @pl.when(i < N_CTX_BLK)
    def _():
        ctx_ref[...] = value

    @pl.when(i >= N_CTX_BLK)
    def _():
        lat_ref[...] = value


def _mods_kernel(c_ref, w_ref, b_ref, o_ref, sb_scr):
    k = pl.program_id(1)

    @pl.when((pl.program_id(0) == 0) & (k == 0))
    def _():
        for v in range(N_GROUPS):
            sb_scr[v] = jnp.broadcast_to(_silu(c_ref[v]), (D_MODEL, LANES))

    @pl.when(k == 0)
    def _():
        for v in range(N_GROUPS):
            o_ref[v] = b_ref[...]

    row_base = k * MOD_ROWS
    zero = jnp.zeros((SUBLANES, LANES), F32)
    for grp in range(6 * D_MODEL // (MOD_TILES * LANES)):
        col0 = grp * MOD_TILES * LANES

        def slab(r, accs, col0=col0):
            r0 = pl.multiple_of(r * SUBLANES, SUBLANES)
            s = [sb_scr[v, pl.ds(row_base + r0, SUBLANES), :] for v in range(N_GROUPS)]
            out = []
            for t in range(MOD_TILES):
                w = w_ref[pl.ds(r0, SUBLANES), col0 + t * LANES:col0 + (t + 1) * LANES]
                out += [accs[t * N_GROUPS + v] + w * s[v] for v in range(N_GROUPS)]
            return tuple(out)

        accs = lax.fori_loop(0, MOD_ROWS // SUBLANES, slab, (zero,) * (MOD_TILES * N_GROUPS),
                             unroll=8)
        for t in range(MOD_TILES):
            cols = slice(col0 + t * LANES, col0 + (t + 1) * LANES)
            for v in range(N_GROUPS):
                o_ref[v, :, cols] += jnp.sum(accs[t * N_GROUPS + v], axis=0, keepdims=True)


def _mods(c_cols, w_mod, b_mod):
    n_out = 6 * D_MODEL
    return pl.pallas_call(
        _mods_kernel,
        grid=(DEPTH, D_MODEL // MOD_ROWS),
        in_specs=[
            pl.BlockSpec((N_GROUPS, D_MODEL, 1), lambda l, k: (0, 0, 0)),
            pl.BlockSpec((None, MOD_ROWS, n_out), lambda l, k: (l, k, 0)),
            pl.BlockSpec((None, 1, n_out), lambda l, k: (l, 0, 0)),
        ],
        out_specs=pl.BlockSpec((None, N_GROUPS, 1, n_out), lambda l, k: (l, 0, 0, 0)),
        out_shape=jax.ShapeDtypeStruct((DEPTH, N_GROUPS, 1, n_out), F32),
        scratch_shapes=[pltpu.VMEM((N_GROUPS, D_MODEL, LANES), F32)],
        compiler_params=_params(("arbitrary", "arbitrary")),
        name="adaln_mods",
    )(c_cols, w_mod, b_mod.reshape(DEPTH, 1, n_out))


def _cumsum2(t, g):
    g1 = g.astype(BF16)
    g2 = (g - g1.astype(F32)).astype(BF16)
    return _dot(t, g1) + _dot(t, g2)


def _rows_bcast(b, first_row):
    n = b.shape[0] // CHUNK
    return jnp.concatenate(
        [jnp.broadcast_to(b[c * CHUNK + first_row:c * CHUNK + first_row + 1, :], (CHUNK, b.shape[1]))
         for c in range(n)], axis=0)


def _gla_in_kernel(xc_ref, xl_ref, mod_ref, ng_ref, w32_ref, wg32_ref, bg_ref, tlow_ref, tup_ref,
                   qf_ref, kf_ref, qb_ref, kb_ref, v_ref, r_ref, oi_ref, df_ref, db_ref,
                   w_ref, wg2_ref):
    i = pl.program_id(0)

    @pl.when(i == 0)
    def _():
        w_ref[:, 0:GLR_COL] = w32_ref[:, 0:GLR_COL].astype(BF16)
        w_ref[:, GLR_COL:GLA_IN_PAD] = jnp.zeros((D_MODEL, GLA_IN_PAD - GLR_COL), BF16)
        w_ref[:, GLR_COL:GLA_IN] = w32_ref[:, GLR_COL:GLA_IN].astype(BF16)
        wg2_ref[...] = jnp.zeros_like(wg2_ref)
        for z in range(2):
            wg2_ref[z, z * GATE_RANK:(z + 1) * GATE_RANK, :] = wg32_ref[z].astype(BF16)

    ri = lax.broadcasted_iota(jnp.int32, (SUPER, SUPER), 0)
    ci = lax.broadcasted_iota(jnp.int32, (SUPER, SUPER), 1)
    same = (ri & -CHUNK) == (ci & -CHUNK)
    low = same & (ci <= ri)
    up = same & (ci >= ri)
    cpg = SUPER // CHUNK
    inv_tau = 1.0 / GATE_TAU

    def project(s, res):
        rows = slice(s * SUPER, (s + 1) * SUPER)
        x = _read_rows(i, xc_ref, xl_ref, rows)
        h = _rms(x, ng_ref[...]) * (1.0 + _mod_part(mod_ref, 1)) + _mod_part(mod_ref, 0)
        hb = h.astype(BF16)
        yield
        qk = _dot(hb, w_ref[:, 0:2 * GLA_QK])
        res["q"] = qk[:, 0:GLA_QK] * (GLA_DK ** -0.5)
        res["k"] = qk[:, GLA_QK:2 * GLA_QK]
        yield
        res["v"] = _dot(hb, w_ref[:, 2 * GLA_QK:2 * GLA_QK + GLA_V]).astype(BF16)
        v_ref[rows, :] = res["v"]
        yield
        r_ref[rows, :] = _dot(hb, w_ref[:, 2 * GLA_QK + GLA_V:GLR_COL])
        yield
        res["glr"] = _dot(hb, w_ref[:, GLR_COL:GLA_IN_PAD]).astype(BF16)

    def local(s, res):
        rows = slice(s * SUPER, (s + 1) * SUPER)
        qs, ks, vb, glr = res["q"], res["k"], res["v"], res["glr"]
        gf = _log_sigmoid(_dot(glr, wg2_ref[0]) + bg_ref[0:1, :]) * inv_tau
        gb = _log_sigmoid(_dot(glr, wg2_ref[1]) + bg_ref[1:2, :]) * inv_tau
        yield
        b = _cumsum2(tlow_ref[...], gf)
        bmid = _rows_bcast(b, CHUNK // 2 - 1)
        blast = _rows_bcast(b, CHUNK - 1)
        qtf = (qs * jnp.exp(b - bmid)).astype(BF16)
        ktf = (ks * jnp.exp(bmid - b)).astype(BF16)
        qf_ref[rows, :] = (qs * jnp.exp(b)).astype(BF16)
        kf_ref[rows, :] = (ks * jnp.exp(blast - b)).astype(BF16)
        for c in range(cpg):
            df_ref[s * cpg + c:s * cpg + c + 1, :] = jnp.exp(b[c * CHUNK + CHUNK - 1:(c + 1) * CHUNK, :])
        yield
        b = _cumsum2(tup_ref[...], gb)
        bmid = _rows_bcast(b, CHUNK // 2)
        blast = _rows_bcast(b, 0)
        qtb = (qs * jnp.exp(b - bmid)).astype(BF16)
        ktb = (ks * jnp.exp(bmid - b)).astype(BF16)
        qb_ref[rows, :] = (qs * jnp.exp(b)).astype(BF16)
        kb_ref[rows, :] = (ks * jnp.exp(blast - b)).astype(BF16)
        for c in range(cpg):
            db_ref[s * cpg + c:s * cpg + c + 1, :] = jnp.exp(b[c * CHUNK:c * CHUNK + 1, :])
        yield
        for hd in range(GLA_HEADS):
            kc = slice(hd * GLA_DK, (hd + 1) * GLA_DK)
            af = _dot_nt(qtf[:, kc], ktf[:, kc])
            ab = _dot_nt(qtb[:, kc], ktb[:, kc])
            a = (jnp.where(low, af, 0.0) + jnp.where(up, ab, 0.0)).astype(BF16)
            vc = slice(hd * GLA_DV, (hd + 1) * GLA_DV)
            oi_ref[rows, vc] = _dot(a, vb[:, vc])

    n_groups = TM // SUPER
    results = [dict() for _ in range(n_groups)]
    for s in range(n_groups + 1):
        stages = []
        if s < n_groups:
            stages.append(project(s, results[s]))
        if s > 0:
            stages.append(local(s - 1, results[s - 1]))
        while stages:
            stages = [g for g in stages if next(g, "done") != "done"]


def _gla_in(xc, xl, mods, layer, norm_g, w_in, w_g2, b_g, tlow, tup):
    gla_layer = layer // 2
    cpb = TM // CHUNK
    row = lambda w: pl.BlockSpec((TM, w), lambda i: (i, 0))
    outs = [
        jax.ShapeDtypeStruct((N_TOK, GLA_QK), BF16),
        jax.ShapeDtypeStruct((N_TOK, GLA_QK), BF16),
        jax.ShapeDtypeStruct((N_TOK, GLA_QK), BF16),
        jax.ShapeDtypeStruct((N_TOK, GLA_QK), BF16),
        jax.ShapeDtypeStruct((N_TOK, GLA_V), BF16),
        jax.ShapeDtypeStruct((N_TOK, GLA_V), F32),
        jax.ShapeDtypeStruct((N_TOK, GLA_V), F32),
        jax.ShapeDtypeStruct((N_TOK // CHUNK, GLA_QK), F32),
        jax.ShapeDtypeStruct((N_TOK // CHUNK, GLA_QK), F32),
    ]
    return pl.pallas_call(
        _gla_in_kernel,
        grid=(N_TOK // TM,),
        in_specs=[
            _ctx_rows(), _lat_rows(), _mod_spec(layer), _layer_spec((1, D_MODEL), layer),
            _layer_spec((D_MODEL, GLA_IN), gla_layer),
            _layer_spec((2, GATE_RANK, GLA_QK), gla_layer), _layer_spec((2, GLA_QK), gla_layer),
            _const_spec((SUPER, SUPER)), _const_spec((SUPER, SUPER)),
        ],
        out_specs=[row(GLA_QK)] * 4 + [row(GLA_V)] * 3
        + [pl.BlockSpec((cpb, GLA_QK), lambda i: (i, 0))] * 2,
        out_shape=outs,
        scratch_shapes=[pltpu.VMEM((D_MODEL, GLA_IN_PAD), BF16),
                        pltpu.VMEM((2, GLA_IN_PAD - GLR_COL, GLA_QK), BF16)],
        compiler_params=_params(("arbitrary",)),
        name="gla_in",
    )(xc, xl, mods, norm_g.reshape(DEPTH, 1, D_MODEL), w_in, w_g2, b_g, tlow, tup)


def _scan_steps(q_ref, k_ref, d_ref, v_ref, chunks, st):
    rows = [pl.multiple_of(ch * CHUNK, CHUNK) for ch in chunks]
    updates = [_dot_tn(v_ref[pl.ds(r0, CHUNK), :], k_ref[pl.ds(r0, CHUNK), :]) for r0 in rows]
    reads = []
    for ch, r0, u in zip(chunks, rows, updates):
        reads.append((r0, _dot_nt(q_ref[pl.ds(r0, CHUNK), :], st.astype(BF16))))
        st = st * d_ref[pl.ds(ch, 1), :] + u
    return st, reads


def _gla_scan_kernel(has_prev, *refs):
    (qf_ref, kf_ref, qb_ref, kb_ref, v_ref, oi_ref, df_ref, db_ref, s0_ref) = refs[:9]
    refs = refs[9:]
    if has_prev:
        prev_ref, refs = refs[0], refs[1:]
    o_ref, st_ref, s_scr = refs
    i = pl.program_id(1)
    o_ref[...] = oi_ref[...]

    def accumulate(reads):
        for r0, val in reads:
            o_ref[pl.ds(r0, CHUNK), :] += val

    @pl.when(i < N_CTX_SCAN)
    def _():
        cps = SEQ // CHUNK
        seq_per_trip = SCAN_STEPS // cps
        layer_slot = N_GLA_LAYERS - 1 if has_prev else 0
        if has_prev:
            for l in range(N_GLA_LAYERS - 1):
                st_ref[:, l] = prev_ref[:, l]

        def trip(t, carry):
            for u in range(seq_per_trip):
                s = t * seq_per_trip + u
                zero = jnp.zeros((GLA_DV, GLA_DK), F32)
                sf, rf = _scan_steps(qf_ref, kf_ref, df_ref, v_ref,
                                     [s * cps + c for c in range(cps)], zero)
                sb, rb = _scan_steps(qb_ref, kb_ref, db_ref, v_ref,
                                     [s * cps + (cps - 1 - c) for c in range(cps)], zero)
                accumulate(rf + rb)
                st_ref[s, layer_slot, 0] = sf.T
                st_ref[s, layer_slot, 1] = sb.T
            return carry

        lax.fori_loop(0, SCAN_ROWS // SEQ // seq_per_trip, trip, 0)

    @pl.when(i >= N_CTX_SCAN)
    def _():
        n_chunks = SCAN_ROWS // CHUNK
        s_scr[0] = s0_ref[0].T
        s_scr[1] = s0_ref[1].T

        def trip(t, carry):
            first = t * SCAN_STEPS
            sf, rf = _scan_steps(qf_ref, kf_ref, df_ref, v_ref,
                                 [first + c for c in range(SCAN_STEPS)], s_scr[0])
            sb, rb = _scan_steps(qb_ref, kb_ref, db_ref, v_ref,
                                 [n_chunks - 1 - first - c for c in range(SCAN_STEPS)], s_scr[1])
            accumulate(rf + rb)
            s_scr[0] = sf
            s_scr[1] = sb
            return carry

        lax.fori_loop(0, n_chunks // SCAN_STEPS, trip, 0)


def _gla_scan(qf, kf, qb, kb, v, oi, df, db, state_gla, gla_layer, prev_states):
    has_prev = prev_states is not None
    n_layers_out = gla_layer + 1
    cpb = SCAN_ROWS // CHUNK
    seq_per_blk = SCAN_ROWS // SEQ
    qk_spec = pl.BlockSpec((SCAN_ROWS, GLA_DK), lambda h, i: (i, h))
    v_spec = pl.BlockSpec((SCAN_ROWS, GLA_DV), lambda h, i: (i, h))
    d_spec = pl.BlockSpec((cpb, GLA_DK), lambda h, i: (i, h))
    ctx_blk = lambda h, i: jnp.minimum(i, N_CTX_SCAN - 1)
    in_specs = [qk_spec, qk_spec, qk_spec, qk_spec, v_spec, v_spec, d_spec, d_spec,
                pl.BlockSpec((None, None, 2, None, GLA_DK, GLA_DV),
                             lambda h, i: (jnp.maximum(i - N_CTX_SCAN, 0), gla_layer, 0, h, 0, 0))]
    args = [qf, kf, qb, kb, v, oi, df, db, state_gla]
    if has_prev:
        in_specs.append(pl.BlockSpec((seq_per_blk, gla_layer, 2, None, GLA_DK, GLA_DV),
                                     lambda h, i: (ctx_blk(h, i), 0, 0, h, 0, 0)))
        args.append(prev_states)
    return pl.pallas_call(
        functools.partial(_gla_scan_kernel, has_prev),
        grid=(GLA_HEADS, N_TOK // SCAN_ROWS),
        in_specs=in_specs,
        out_specs=[
            v_spec,
            pl.BlockSpec((seq_per_blk, n_layers_out, 2, None, GLA_DK, GLA_DV),
                         lambda h, i: (ctx_blk(h, i), 0, 0, h, 0, 0)),
        ],
        out_shape=[jax.ShapeDtypeStruct((N_TOK, GLA_V), F32),
                   jax.ShapeDtypeStruct((BATCH, n_layers_out, 2, GLA_HEADS, GLA_DK, GLA_DV), F32)],
        scratch_shapes=[pltpu.VMEM((2, GLA_DV, GLA_DK), F32)],
        compiler_params=_params(("arbitrary", "arbitrary")),
        name="gla_scan",
    )(*args)


def _fnet_in_kernel(xc_ref, xl_ref, mod_ref, ng_ref, w32_ref, cs_ref, c_ref, s_ref, uc_ref, us_ref,
                    w_ref):
    i = pl.program_id(0)

    @pl.when(i == 0)
    def _():
        w_ref[...] = w32_ref[...].astype(BF16)

    x = _read_rows(i, xc_ref, xl_ref)
    h = _rms(x, ng_ref[...]) * (1.0 + _mod_part(mod_ref, 1)) + _mod_part(mod_ref, 0)
    u = _dot(h.astype(BF16), w_ref[...]).astype(BF16)
    for g in range(FNET_GROUPS):
        cols = slice(g * FNET_GW, (g + 1) * FNET_GW)
        t = _dot(u[:, cols], cs_ref[...])
        uc_ref[:, cols] = t[:, 0:FNET_GW].astype(BF16)
        us_ref[:, cols] = t[:, FNET_GW:2 * FNET_GW].astype(BF16)

    @pl.when(i < N_CTX_BLK)
    def _():
        for s in range(TM // SEQ):
            rows = slice(s * SEQ, (s + 1) * SEQ)
            f = _dot(c_ref[...], uc_ref[rows, :]) - _dot(s_ref[...], us_ref[rows, :])
            uc_ref[rows, :] = (f * SEQ ** -0.5).astype(BF16)


def _fnet_in(xc, xl, mods, layer, norm_g, w_in, cs_chan, c_tab, s_tab):
    row = pl.BlockSpec((TM, D_MODEL), lambda i: (i, 0))
    return pl.pallas_call(
        _fnet_in_kernel,
        grid=(N_TOK // TM,),
        in_specs=[_ctx_rows(), _lat_rows(), _mod_spec(layer), _layer_spec((1, D_MODEL), layer),
                  _layer_spec((D_MODEL, D_MODEL), layer // 2), _const_spec((FNET_GW, 2 * FNET_GW)),
                  _const_spec((SEQ, SEQ)), _const_spec((SEQ, SEQ))],
        out_specs=[row, row],
        out_shape=[jax.ShapeDtypeStruct((N_TOK, D_MODEL), BF16)] * 2,
        scratch_shapes=[pltpu.VMEM((D_MODEL, D_MODEL), BF16)],
        compiler_params=_params(("arbitrary",)),
        name="fnet_in",
    )(xc, xl, mods, norm_g.reshape(DEPTH, 1, D_MODEL), w_in, cs_chan, c_tab, s_tab)


HALF_SEQ = DEC_SEQ // 2


def _dft_lat_kernel(uc_ref, us_ref, c_ref, s_ref, tw_ref, f_ref, ec_scr, es_scr, oc_scr, os_scr):
    @pl.when(pl.program_id(1) == 0)
    def _():
        lo, hi = slice(0, HALF_SEQ), slice(HALF_SEQ, DEC_SEQ)
        c_lo, c_hi = uc_ref[lo, :].astype(F32), uc_ref[hi, :].astype(F32)
        s_lo, s_hi = us_ref[lo, :].astype(F32), us_ref[hi, :].astype(F32)
        ec_scr[...] = (c_lo + c_hi).astype(BF16)
        es_scr[...] = (s_lo + s_hi).astype(BF16)
        dc, ds = c_lo - c_hi, s_lo - s_hi
        cos_l, sin_l = tw_ref[:, 0:1], tw_ref[:, 1:2]
        oc_scr[...] = (cos_l * dc - sin_l * ds).astype(BF16)
        os_scr[...] = (sin_l * dc + cos_l * ds).astype(BF16)

    scale = DEC_SEQ ** -0.5
    even = (_dot(c_ref[...], ec_scr[...]) - _dot(s_ref[...], es_scr[...])) * scale
    odd = (_dot(c_ref[...], oc_scr[...]) - _dot(s_ref[...], os_scr[...])) * scale
    for t in range(D_MODEL // LANES):
        cols = slice(t * LANES, (t + 1) * LANES)
        f_ref[t, pl.ds(0, TM, stride=2), :] = even[:, cols]
        f_ref[t, pl.ds(1, TM, stride=2), :] = odd[:, cols]


def _fnet_dft_lat(uc, us, c_half, s_half, twiddle):
    n_ctx_blk = N_CTX_TOK // DEC_SEQ
    m_tiles = HALF_SEQ // TM
    seq_blk = pl.BlockSpec((DEC_SEQ, D_MODEL), lambda b, m: (n_ctx_blk + b, 0))
    tab = pl.BlockSpec((TM, HALF_SEQ), lambda b, m: (m, 0))
    return pl.pallas_call(
        _dft_lat_kernel,
        grid=(DEC_BATCH, m_tiles),
        in_specs=[seq_blk, seq_blk, tab, tab, _const_spec((HALF_SEQ, 2))],
        out_specs=pl.BlockSpec((D_MODEL // LANES, 2 * TM, LANES),
                               lambda b, m: (0, b * m_tiles + m, 0)),
        out_shape=jax.ShapeDtypeStruct((D_MODEL // LANES, N_LAT_TOK, LANES), F32),
        scratch_shapes=[pltpu.VMEM((HALF_SEQ, D_MODEL), BF16)] * 4,
        compiler_params=_params(("arbitrary", "arbitrary")),
        name="fnet_dft_lat",
    )(uc, us, c_half, s_half, twiddle)


def _cast_kernel(w_ref, o_ref):
    o_ref[...] = w_ref[...].astype(BF16)


def _ffn_up_weights(w_up):
    n_f = FFN_DIM // TF
    return pl.pallas_call(
        _cast_kernel,
        grid=(DEPTH, n_f, 2),
        in_specs=[pl.BlockSpec((None, D_MODEL, TF), lambda l, t, part: (l, 0, part * n_f + t))],
        out_specs=pl.BlockSpec((None, D_MODEL, TF), lambda l, t, part: (l, 0, 2 * t + part)),
        out_shape=jax.ShapeDtypeStruct((DEPTH, D_MODEL, 2 * FFN_DIM), BF16),
        compiler_params=_params(("arbitrary",) * 3),
        name="ffn_up_cast",
    )(w_up)


def _ffn_down_weights(w_down):
    return pl.pallas_call(
        _cast_kernel,
        grid=(DEPTH, FFN_DIM // TF),
        in_specs=[pl.BlockSpec((None, TF, D_MODEL), lambda l, t: (l, t, 0))],
        out_specs=pl.BlockSpec((None, TF, D_MODEL), lambda l, t: (l, t, 0)),
        out_shape=jax.ShapeDtypeStruct((DEPTH, FFN_DIM, D_MODEL), BF16),
        compiler_params=_params(("arbitrary",) * 2),
        name="ffn_down_cast",
    )(w_down)


def _mix_ffn_kernel(is_gla, is_last, *refs):
    xc_ref, xl_ref = refs[:2]
    if is_gla:
        o_ref, r_ref, gn_ref = refs[2:5]
        refs = refs[5:]
    else:
        fc_ref, fl_ref = refs[2:4]
        refs = refs[4:]
    (wo32_ref, mod_ref, ng_ref, wu_ref, cw_ref, cb_ref, wd_ref, fg_ref,
     outc_ref, outl_ref, x_scr, h_scr, acc_scr, out_scr, wo_ref) = refs
    i = pl.program_id(0)
    j = pl.program_id(1)
    n_f = FFN_DIM // TF

    @pl.when((i == 0) & (j == 0))
    def _():
        wo_ref[...] = wo32_ref[...].astype(BF16)

    period = jnp.where(i < N_CTX_BLK, SEQ, GRID_W)
    pos = lax.broadcasted_iota(jnp.int32, (HALF, 1), 0) & (period - 1)
    has_prev = (pos != 0).astype(F32)
    has_next = (pos != period - 1).astype(F32)

    def mixer_out(rows):
        if is_gla:
            parts = []
            for hd in range(GLA_HEADS):
                cols = slice(hd * GLA_DV, (hd + 1) * GLA_DV)
                y = _rms(o_ref[rows, cols], gn_ref[...]) * _silu(r_ref[rows, cols])
                parts.append(y.astype(BF16))
            y = jnp.concatenate(parts, axis=1)
        else:
            f_lat = jnp.concatenate([fl_ref[t, rows, :] for t in range(D_MODEL // LANES)], axis=1)
            y = jnp.where(i < N_CTX_BLK, fc_ref[rows, :], f_lat.astype(BF16))
        x_new = _read_rows(i, xc_ref, xl_ref, rows) + _mod_part(mod_ref, 2) * _dot(y, wo_ref[...])
        x_scr[rows, :] = x_new
        h = _rms(x_new, ng_ref[...]) * (1.0 + _mod_part(mod_ref, 4)) + _mod_part(mod_ref, 3)
        hb = h.astype(BF16)
        h_scr[rows, :] = hb
        return hb

    def ffn_tile(hb):
        ag = _dot(hb, wu_ref[...])
        a = ag[:, 0:TF]
        g = ag[:, TF:2 * TF]
        a_prev = pltpu.roll(a, 1, 0) * has_prev
        a_next = pltpu.roll(a, HALF - 1, 0) * has_next
        conv = a_prev * cw_ref[0:1, :] + a * cw_ref[1:2, :] + a_next * cw_ref[2:3, :] + cb_ref[...]
        return _dot((_silu(conv) * g).astype(BF16), wd_ref[...])

    def step(first, last):
        for s in range(TM // HALF):
            rows = slice(s * HALF, (s + 1) * HALF)
            hb = mixer_out(rows) if first else h_scr[rows, :]
            d = ffn_tile(hb)
            if not first:
                d = acc_scr[rows, :] + d
            if last:
                out = x_scr[rows, :] + _mod_part(mod_ref, 5) * d
                if is_last:
                    out = _rms(out, fg_ref[...])
                out_scr[rows, :] = out
            else:
                acc_scr[rows, :] = d

    pl.when(j == 0)(lambda: step(True, n_f == 1))
    if n_f > 2:
        pl.when((j > 0) & (j < n_f - 1))(lambda: step(False, False))
    if n_f > 1:
        pl.when(j == n_f - 1)(lambda: step(False, True))

    @pl.when(j == n_f - 1)
    def _():
        _write_rows(i, outc_ref, outl_ref, out_scr[...])


def _mix_ffn(is_gla, is_last, xc, xl, mix_in, w_o, mods, layer, norm_g, w_up, conv_w, conv_b,
             w_down, final_g):
    n_f = FFN_DIM // TF
    row = pl.BlockSpec((TM, D_MODEL), lambda i, j: (i, 0))
    if is_gla:
        mix_specs = [row, row, _layer_spec((1, GLA_DV), layer // 2)]
    else:
        mix_specs = [_ctx_rows(),
                     pl.BlockSpec((D_MODEL // LANES, TM, LANES),
                                  lambda i, j: (0, jnp.maximum(i - N_CTX_BLK, 0), 0))]
    in_specs = [_ctx_rows(), _lat_rows()] + mix_specs + [
        _layer_spec((D_MODEL, D_MODEL), layer // 2), _mod_spec(layer),
        _layer_spec((1, D_MODEL), layer),
        pl.BlockSpec((None, D_MODEL, 2 * TF), lambda i, j: (layer, 0, j)),
        pl.BlockSpec((None, 3, TF), lambda i, j: (layer, 0, j)),
        pl.BlockSpec((None, 1, TF), lambda i, j: (layer, 0, j)),
        pl.BlockSpec((None, TF, D_MODEL), lambda i, j: (layer, j, 0)),
        _const_spec((1, D_MODEL)),
    ]
    return pl.pallas_call(
        functools.partial(_mix_ffn_kernel, is_gla, is_last),
        grid=(N_TOK // TM, FFN_DIM // TF),
        in_specs=in_specs,
        out_specs=[_ctx_rows(), _lat_rows()],
        out_shape=[jax.ShapeDtypeStruct((N_CTX_TOK, D_MODEL), F32),
                   jax.ShapeDtypeStruct((N_LAT_TOK, D_MODEL), F32)],
        scratch_shapes=[pltpu.VMEM((TM, D_MODEL), F32), pltpu.VMEM((TM, D_MODEL), BF16),
                        pltpu.VMEM((TM, D_MODEL), F32), pltpu.VMEM((TM, D_MODEL), F32),
                        pltpu.VMEM((D_MODEL, D_MODEL), BF16)],
        compiler_params=_params(("arbitrary", "arbitrary")),
        name="gla_out_ffn" if is_gla else "fnet_out_ffn",
    )(xc, xl, *mix_in, w_o, mods, norm_g.reshape(DEPTH, 1, D_MODEL), w_up, conv_w,
      conv_b.reshape(DEPTH, 1, FFN_DIM), w_down, final_g)


def _dft_tables(n, scale):
    idx = np.arange(n, dtype=np.int64)
    ang = (2.0 * np.pi / n) * ((idx[:, None] * idx[None, :]) % n).astype(np.float64)
    return (np.cos(ang) * scale).astype(np.float32), (np.sin(ang) * scale).astype(np.float32)


def _chunk_triangles():
    idx = np.arange(SUPER)
    same = (idx[:, None] // CHUNK) == (idx[None, :] // CHUNK)
    low = same & (idx[None, :] <= idx[:, None])
    up = same & (idx[None, :] >= idx[:, None])
    return low.astype(np.float32), up.astype(np.float32)


def kernel(x_prompt, x_sample, state_gla, c, c_ctx, norm_mix_g, norm_ffn_g, w_mod, b_mod,
           gla_w_in, gla_w_g2, gla_b_g, gla_norm_g, gla_w_o, fnet_w_in, fnet_w_o,
           ffn_w_up, ffn_conv_w, ffn_conv_b, ffn_w_down, final_norm_g):
    xc = x_prompt.reshape(N_CTX_TOK, D_MODEL)
    xl = x_sample.reshape(N_LAT_TOK, D_MODEL)
    c_cols = jnp.concatenate([c_ctx[None, :], c], axis=0)[:, :, None]
    mods = _mods(c_cols, w_mod, b_mod)

    tlow_np, tup_np = _chunk_triangles()
    tlow, tup = jnp.asarray(tlow_np).astype(BF16), jnp.asarray(tup_np).astype(BF16)
    cc, sc = _dft_tables(FNET_GW, FNET_GW ** -0.5)
    cs_chan = jnp.asarray(np.concatenate([cc, sc], axis=1)).astype(BF16)
    c_ctx_tab, s_ctx_tab = (jnp.asarray(t).astype(BF16) for t in _dft_tables(SEQ, 1.0))
    c_lat_tab, s_lat_tab = (jnp.asarray(t).astype(BF16) for t in _dft_tables(HALF_SEQ, 1.0))
    phase = (2.0 * np.pi / DEC_SEQ) * np.arange(HALF_SEQ, dtype=np.float64)
    twiddle = jnp.asarray(np.stack([np.cos(phase), np.sin(phase)], axis=1).astype(np.float32))
    final_g = final_norm_g.reshape(1, D_MODEL)

    w_up = _ffn_up_weights(ffn_w_up)
    w_down = _ffn_down_weights(ffn_w_down)
    head_norm_g = gla_norm_g.reshape(N_GLA_LAYERS, 1, GLA_DV)

    states = None
    for layer in range(DEPTH):
        j = layer // 2
        if layer % 2 == 0:
            qf, kf, qb, kb, v, r, oi, df, db = _gla_in(
                xc, xl, mods, layer, norm_mix_g, gla_w_in, gla_w_g2, gla_b_g, tlow, tup)
            o, states = _gla_scan(qf, kf, qb, kb, v, oi, df, db, state_gla, j, states)
            mix_in = (o, r, head_norm_g)
            w_o = gla_w_o
        else:
            uc, us = _fnet_in(xc, xl, mods, layer, norm_mix_g, fnet_w_in, cs_chan,
                              c_ctx_tab, s_ctx_tab)
            f_lat = _fnet_dft_lat(uc, us, c_lat_tab, s_lat_tab, twiddle)
            mix_in = (uc, f_lat)
            w_o = fnet_w_o
        xc, xl = _mix_ffn(layer % 2 == 0, layer == DEPTH - 1, xc, xl, mix_in, w_o, mods, layer,
                          norm_ffn_g, w_up, ffn_conv_w, ffn_conv_b, w_down, final_g)

    return (xc.reshape(BATCH, SEQ, D_MODEL), xl.reshape(DEC_BATCH, DEC_SEQ, D_MODEL), states)
```

```python
import functools

import numpy as np
import jax
import jax.numpy as jnp
from jax import lax
from jax.experimental import pallas as pl
from jax.experimental.pallas import tpu as pltpu

F32 = jnp.float32
BF16 = jnp.bfloat16

D_MODEL = 1024
BATCH = 16
SEQ = 256
DEPTH = 4
DEC_BATCH = 2
DEC_SEQ = 2048
GRID_W = 64
N_GLA_LAYERS = 2
GLA_HEADS = 4
GLA_QK = D_MODEL // 2
GLA_V = D_MODEL
GLA_DK = GLA_QK // GLA_HEADS
GLA_DV = GLA_V // GLA_HEADS
GATE_RANK = 16
GATE_TAU = 16.0
GLA_IN = 2 * GLA_QK + 2 * GLA_V + 2 * GATE_RANK
FNET_GROUPS = 4
FNET_GW = D_MODEL // FNET_GROUPS
FFN_DIM = 2816
EPS = 1e-6

N_CTX_TOK = BATCH * SEQ
N_LAT_TOK = DEC_BATCH * DEC_SEQ
N_TOK = N_CTX_TOK + N_LAT_TOK
N_GROUPS = 1 + DEC_BATCH

LANES = 128
SUBLANES = 8
CHUNK = 64
SUPER = 256
TM = 512
N_CTX_BLK = N_CTX_TOK // TM
SCAN_ROWS = 2048
N_CTX_SCAN = N_CTX_TOK // SCAN_ROWS
SCAN_STEPS = 16
HALF = 256
TF = 1408
GLA_IN_PAD = 3328
GLR_COL = 2 * GLA_QK + 2 * GLA_V
VMEM_LIMIT = 56 * 1024 * 1024
MOD_ROWS = 512
MOD_TILES = 8


def _dot(a, b):
    return jnp.dot(a, b, preferred_element_type=F32)


def _dot_nt(a, b):
    return lax.dot_general(a, b, (((1,), (1,)), ((), ())), preferred_element_type=F32)


def _dot_tn(a, b):
    return lax.dot_general(a, b, (((0,), (0,)), ((), ())), preferred_element_type=F32)


def _silu(x):
    return x / (1.0 + jnp.exp(-x))


def _log_sigmoid(x):
    return jnp.minimum(x, 0.0) - jnp.log(1.0 + jnp.exp(-jnp.abs(x)))


def _rms(x, g):
    return x * lax.rsqrt(jnp.mean(x * x, axis=-1, keepdims=True) + EPS) * g


def _group_of_block(i):
    return jnp.where(i < N_CTX_BLK, 0, 1 + (i - N_CTX_BLK) // (DEC_SEQ // TM))


def _params(sem):
    return pltpu.CompilerParams(dimension_semantics=sem, vmem_limit_bytes=VMEM_LIMIT)


def _const_spec(shape):
    nd = len(shape)
    return pl.BlockSpec(shape, lambda *_: (0,) * nd, pipeline_mode=pl.Buffered(1))


def _layer_spec(shape, layer):
    nd = len(shape)
    return pl.BlockSpec((None,) + tuple(shape), lambda *_: (layer,) + (0,) * nd,
                        pipeline_mode=pl.Buffered(1))


def _mod_spec(layer):
    return pl.BlockSpec((None, None, 1, 6 * D_MODEL),
                        lambda i, *_: (layer, _group_of_block(i), 0, 0))


def _mod_part(mod_ref, k):
    return mod_ref[:, k * D_MODEL:(k + 1) * D_MODEL]


def _ctx_rows(width=D_MODEL):
    return pl.BlockSpec((TM, width), lambda i, *_: (jnp.minimum(i, N_CTX_BLK - 1), 0))


def _lat_rows(width=D_MODEL):
    return pl.BlockSpec((TM, width), lambda i, *_: (jnp.maximum(i - N_CTX_BLK, 0), 0))


def _read_rows(i, ctx_ref, lat_ref, rows=slice(None)):
    return jnp.where(i < N_CTX_BLK, ctx_ref[rows, :], lat_ref[rows, :])


def _write_rows(i, ctx_ref, lat_ref, value):
    @pl.when(i < N_CTX_BLK)
    def _():
        ctx_ref[...] = value

    @pl.when(i >= N_CTX_BLK)
    def _():
        lat_ref[...] = value


def _mods_kernel(c_ref, w_ref, b_ref, o_ref, sb_scr):
    k = pl.program_id(1)

    @pl.when((pl.program_id(0) == 0) & (k == 0))
    def _():
        for v in range(N_GROUPS):
            sb_scr[v] = jnp.broadcast_to(_silu(c_ref[v]), (D_MODEL, LANES))

    @pl.when(k == 0)
    def _():
        for v in range(N_GROUPS):
            o_ref[v] = b_ref[...]

    row_base = k * MOD_ROWS
    zero = jnp.zeros((SUBLANES, LANES), F32)
    for grp in range(6 * D_MODEL // (MOD_TILES * LANES)):
        col0 = grp * MOD_TILES * LANES

        def slab(r, accs, col0=col0):
            r0 = pl.multiple_of(r * SUBLANES, SUBLANES)
            s = [sb_scr[v, pl.ds(row_base + r0, SUBLANES), :] for v in range(N_GROUPS)]
            out = []
            for t in range(MOD_TILES):
                w = w_ref[pl.ds(r0, SUBLANES), col0 + t * LANES:col0 + (t + 1) * LANES]
                out += [accs[t * N_GROUPS + v] + w * s[v] for v in range(N_GROUPS)]
            return tuple(out)

        accs = lax.fori_loop(0, MOD_ROWS // SUBLANES, slab, (zero,) * (MOD_TILES * N_GROUPS),
                             unroll=8)
        for t in range(MOD_TILES):
            cols = slice(col0 + t * LANES, col0 + (t + 1) * LANES)
            for v in range(N_GROUPS):
                o_ref[v, :, cols] += jnp.sum(accs[t * N_GROUPS + v], axis=0, keepdims=True)


def _mods(c_cols, w_mod, b_mod):
    n_out = 6 * D_MODEL
    return pl.pallas_call(
        _mods_kernel,
        grid=(DEPTH, D_MODEL // MOD_ROWS),
        in_specs=[
            pl.BlockSpec((N_GROUPS, D_MODEL, 1), lambda l, k: (0, 0, 0)),
            pl.BlockSpec((None, MOD_ROWS, n_out), lambda l, k: (l, k, 0)),
            pl.BlockSpec((None, 1, n_out), lambda l, k: (l, 0, 0)),
        ],
        out_specs=pl.BlockSpec((None, N_GROUPS, 1, n_out), lambda l, k: (l, 0, 0, 0)),
        out_shape=jax.ShapeDtypeStruct((DEPTH, N_GROUPS, 1, n_out), F32),
        scratch_shapes=[pltpu.VMEM((N_GROUPS, D_MODEL, LANES), F32)],
        compiler_params=_params(("arbitrary", "arbitrary")),
        name="adaln_mods",
    )(c_cols, w_mod, b_mod.reshape(DEPTH, 1, n_out))


def _cumsum2(t, g):
    g1 = g.astype(BF16)
    g2 = (g - g1.astype(F32)).astype(BF16)
    return _dot(t, g1) + _dot(t, g2)


def _rows_bcast(b, first_row):
    n = b.shape[0] // CHUNK
    return jnp.concatenate(
        [jnp.broadcast_to(b[c * CHUNK + first_row:c * CHUNK + first_row + 1, :], (CHUNK, b.shape[1]))
         for c in range(n)], axis=0)


def _store_heads(ref, rows, value):
    width = ref.shape[-1]
    for hd in range(ref.shape[0]):
        ref[hd, rows, :] = value[:, hd * width:(hd + 1) * width]


def _gla_in_kernel(xc_ref, xl_ref, mod_ref, ng_ref, w32_ref, wg32_ref, bg_ref, tlow_ref, tup_ref,
                   qf_ref, kf_ref, qb_ref, kb_ref, v_ref, r_ref, oi_ref, df_ref, db_ref,
                   w_ref, wg2_ref):
    i = pl.program_id(0)

    @pl.when(i == 0)
    def _():
        w_ref[:, 0:GLR_COL] = w32_ref[:, 0:GLR_COL].astype(BF16)
        w_ref[:, GLR_COL:GLA_IN_PAD] = jnp.zeros((D_MODEL, GLA_IN_PAD - GLR_COL), BF16)
        w_ref[:, GLR_COL:GLA_IN] = w32_ref[:, GLR_COL:GLA_IN].astype(BF16)
        wg2_ref[...] = jnp.zeros_like(wg2_ref)
        for z in range(2):
            wg2_ref[z, z * GATE_RANK:(z + 1) * GATE_RANK, :] = wg32_ref[z].astype(BF16)

    ri = lax.broadcasted_iota(jnp.int32, (SUPER, SUPER), 0)
    ci = lax.broadcasted_iota(jnp.int32, (SUPER, SUPER), 1)
    same = (ri & -CHUNK) == (ci & -CHUNK)
    low = same & (ci <= ri)
    up = same & (ci >= ri)
    cpg = SUPER // CHUNK
    inv_tau = 1.0 / GATE_TAU

    def project(s, res):
        rows = slice(s * SUPER, (s + 1) * SUPER)
        x = _read_rows(i, xc_ref, xl_ref, rows)
        h = _rms(x, ng_ref[...]) * (1.0 + _mod_part(mod_ref, 1)) + _mod_part(mod_ref, 0)
        hb = h.astype(BF16)
        yield
        qk = _dot(hb, w_ref[:, 0:2 * GLA_QK])
        res["q"] = qk[:, 0:GLA_QK] * (GLA_DK ** -0.5)
        res["k"] = qk[:, GLA_QK:2 * GLA_QK]
        yield
        res["v"] = _dot(hb, w_ref[:, 2 * GLA_QK:2 * GLA_QK + GLA_V]).astype(BF16)
        _store_heads(v_ref, rows, res["v"])
        yield
        r_ref[rows, :] = _dot(hb, w_ref[:, 2 * GLA_QK + GLA_V:GLR_COL])
        yield
        res["glr"] = _dot(hb, w_ref[:, GLR_COL:GLA_IN_PAD]).astype(BF16)

    def local(s, res):
        rows = slice(s * SUPER, (s + 1) * SUPER)
        qs, ks, vb, glr = res["q"], res["k"], res["v"], res["glr"]
        gf = _log_sigmoid(_dot(glr, wg2_ref[0]) + bg_ref[0:1, :]) * inv_tau
        gb = _log_sigmoid(_dot(glr, wg2_ref[1]) + bg_ref[1:2, :]) * inv_tau
        yield
        b = _cumsum2(tlow_ref[...], gf)
        bmid = _rows_bcast(b, CHUNK // 2 - 1)
        blast = _rows_bcast(b, CHUNK - 1)
        qtf = (qs * jnp.exp(b - bmid)).astype(BF16)
        ktf = (ks * jnp.exp(bmid - b)).astype(BF16)
        _store_heads(qf_ref, rows, (qs * jnp.exp(b)).astype(BF16))
        _store_heads(kf_ref, rows, (ks * jnp.exp(blast - b)).astype(BF16))
        for c in range(cpg):
            df_ref[s * cpg + c:s * cpg + c + 1, :] = jnp.exp(b[c * CHUNK + CHUNK - 1:(c + 1) * CHUNK, :])
        yield
        b = _cumsum2(tup_ref[...], gb)
        bmid = _rows_bcast(b, CHUNK // 2)
        blast = _rows_bcast(b, 0)
        qtb = (qs * jnp.exp(b - bmid)).astype(BF16)
        ktb = (ks * jnp.exp(bmid - b)).astype(BF16)
        _store_heads(qb_ref, rows, (qs * jnp.exp(b)).astype(BF16))
        _store_heads(kb_ref, rows, (ks * jnp.exp(blast - b)).astype(BF16))
        for c in range(cpg):
            db_ref[s * cpg + c:s * cpg + c + 1, :] = jnp.exp(b[c * CHUNK:c * CHUNK + 1, :])
        yield
        for hd in range(GLA_HEADS):
            kc = slice(hd * GLA_DK, (hd + 1) * GLA_DK)
            af = _dot_nt(qtf[:, kc], ktf[:, kc])
            ab = _dot_nt(qtb[:, kc], ktb[:, kc])
            a = (jnp.where(low, af, 0.0) + jnp.where(up, ab, 0.0)).astype(BF16)
            vc = slice(hd * GLA_DV, (hd + 1) * GLA_DV)
            oi_ref[hd, rows, :] = _dot(a, vb[:, vc])

    n_groups = TM // SUPER
    results = [dict() for _ in range(n_groups)]
    for s in range(n_groups + 1):
        stages = []
        if s < n_groups:
            stages.append(project(s, results[s]))
        if s > 0:
            stages.append(local(s - 1, results[s - 1]))
        while stages:
            stages = [g for g in stages if next(g, "done") != "done"]


def _gla_in(xc, xl, mods, layer, norm_g, w_in, w_g2, b_g, tlow, tup):
    gla_layer = layer // 2
    cpb = TM // CHUNK
    row = lambda w: pl.BlockSpec((TM, w), lambda i: (i, 0))
    heads = lambda w: pl.BlockSpec((GLA_HEADS, TM, w), lambda i: (0, i, 0))
    outs = [
        jax.ShapeDtypeStruct((GLA_HEADS, N_TOK, GLA_DK), BF16),
        jax.ShapeDtypeStruct((GLA_HEADS, N_TOK, GLA_DK), BF16),
        jax.ShapeDtypeStruct((GLA_HEADS, N_TOK, GLA_DK), BF16),
        jax.ShapeDtypeStruct((GLA_HEADS, N_TOK, GLA_DK), BF16),
        jax.ShapeDtypeStruct((GLA_HEADS, N_TOK, GLA_DV), BF16),
        jax.ShapeDtypeStruct((N_TOK, GLA_V), F32),
        jax.ShapeDtypeStruct((GLA_HEADS, N_TOK, GLA_DV), F32),
        jax.ShapeDtypeStruct((N_TOK // CHUNK, GLA_QK), F32),
        jax.ShapeDtypeStruct((N_TOK // CHUNK, GLA_QK), F32),
    ]
    return pl.pallas_call(
        _gla_in_kernel,
        grid=(N_TOK // TM,),
        in_specs=[
            _ctx_rows(), _lat_rows(), _mod_spec(layer), _layer_spec((1, D_MODEL), layer),
            _layer_spec((D_MODEL, GLA_IN), gla_layer),
            _layer_spec((2, GATE_RANK, GLA_QK), gla_layer), _layer_spec((2, GLA_QK), gla_layer),
            _const_spec((SUPER, SUPER)), _const_spec((SUPER, SUPER)),
        ],
        out_specs=[heads(GLA_DK)] * 4 + [heads(GLA_DV), row(GLA_V), heads(GLA_DV)]
        + [pl.BlockSpec((cpb, GLA_QK), lambda i: (i, 0))] * 2,
        out_shape=outs,
        scratch_shapes=[pltpu.VMEM((D_MODEL, GLA_IN_PAD), BF16),
                        pltpu.VMEM((2, GLA_IN_PAD - GLR_COL, GLA_QK), BF16)],
        compiler_params=_params(("arbitrary",)),
        name="gla_in",
    )(xc, xl, mods, norm_g.reshape(DEPTH, 1, D_MODEL), w_in, w_g2, b_g, tlow, tup)


def _scan_steps(q_ref, k_ref, d_ref, v_ref, chunks, st):
    rows = [pl.multiple_of(ch * CHUNK, CHUNK) for ch in chunks]
    updates = [_dot_tn(v_ref[pl.ds(r0, CHUNK), :], k_ref[pl.ds(r0, CHUNK), :]) for r0 in rows]
    reads = []
    for ch, r0, u in zip(chunks, rows, updates):
        reads.append((r0, _dot_nt(q_ref[pl.ds(r0, CHUNK), :], st.astype(BF16))))
        st = st * d_ref[pl.ds(ch, 1), :] + u
    return st, reads


def _gla_scan_kernel(has_prev, *refs):
    (qf_ref, kf_ref, qb_ref, kb_ref, v_ref, oi_ref, df_ref, db_ref, s0_ref) = refs[:9]
    refs = refs[9:]
    if has_prev:
        prev_ref, refs = refs[0], refs[1:]
    o_ref, st_ref, s_scr = refs
    i = pl.program_id(1)
    o_ref[...] = oi_ref[...]

    def accumulate(reads):
        for r0, val in reads:
            o_ref[pl.ds(r0, CHUNK), :] += val

    @pl.when(i < N_CTX_SCAN)
    def _():
        cps = SEQ // CHUNK
        seq_per_trip = SCAN_STEPS // cps
        layer_slot = N_GLA_LAYERS - 1 if has_prev else 0
        if has_prev:
            for l in range(N_GLA_LAYERS - 1):
                st_ref[:, l] = prev_ref[:, l]

        def trip(t, carry):
            for u in range(seq_per_trip):
                s = t * seq_per_trip + u
                zero = jnp.zeros((GLA_DV, GLA_DK), F32)
                sf, rf = _scan_steps(qf_ref, kf_ref, df_ref, v_ref,
                                     [s * cps + c for c in range(cps)], zero)
                sb, rb = _scan_steps(qb_ref, kb_ref, db_ref, v_ref,
                                     [s * cps + (cps - 1 - c) for c in range(cps)], zero)
                accumulate(rf + rb)
                st_ref[s, layer_slot, 0] = sf.T
                st_ref[s, layer_slot, 1] = sb.T
            return carry

        lax.fori_loop(0, SCAN_ROWS // SEQ // seq_per_trip, trip, 0)

    @pl.when(i >= N_CTX_SCAN)
    def _():
        n_chunks = SCAN_ROWS // CHUNK
        s_scr[0] = s0_ref[0].T
        s_scr[1] = s0_ref[1].T

        def trip(t, carry):
            first = t * SCAN_STEPS
            sf, rf = _scan_steps(qf_ref, kf_ref, df_ref, v_ref,
                                 [first + c for c in range(SCAN_STEPS)], s_scr[0])
            sb, rb = _scan_steps(qb_ref, kb_ref, db_ref, v_ref,
                                 [n_chunks - 1 - first - c for c in range(SCAN_STEPS)], s_scr[1])
            accumulate(rf + rb)
            s_scr[0] = sf
            s_scr[1] = sb
            return carry

        lax.fori_loop(0, n_chunks // SCAN_STEPS, trip, 0)


def _gla_scan(qf, kf, qb, kb, v, oi, df, db, state_gla, gla_layer, prev_states):
    has_prev = prev_states is not None
    n_layers_out = gla_layer + 1
    cpb = SCAN_ROWS // CHUNK
    seq_per_blk = SCAN_ROWS // SEQ
    qk_spec = pl.BlockSpec((None, SCAN_ROWS, GLA_DK), lambda h, i: (h, i, 0))
    v_spec = pl.BlockSpec((None, SCAN_ROWS, GLA_DV), lambda h, i: (h, i, 0))
    d_spec = pl.BlockSpec((cpb, GLA_DK), lambda h, i: (i, h))
    ctx_blk = lambda h, i: jnp.minimum(i, N_CTX_SCAN - 1)
    in_specs = [qk_spec, qk_spec, qk_spec, qk_spec, v_spec, v_spec, d_spec, d_spec,
                pl.BlockSpec((None, None, 2, None, GLA_DK, GLA_DV),
                             lambda h, i: (jnp.maximum(i - N_CTX_SCAN, 0), gla_layer, 0, h, 0, 0))]
    args = [qf, kf, qb, kb, v, oi, df, db, state_gla]
    if has_prev:
        in_specs.append(pl.BlockSpec((seq_per_blk, gla_layer, 2, None, GLA_DK, GLA_DV),
                                     lambda h, i: (ctx_blk(h, i), 0, 0, h, 0, 0)))
        args.append(prev_states)
    return pl.pallas_call(
        functools.partial(_gla_scan_kernel, has_prev),
        grid=(GLA_HEADS, N_TOK // SCAN_ROWS),
        in_specs=in_specs,
        out_specs=[
            v_spec,
            pl.BlockSpec((seq_per_blk, n_layers_out, 2, None, GLA_DK, GLA_DV),
                         lambda h, i: (ctx_blk(h, i), 0, 0, h, 0, 0)),
        ],
        out_shape=[jax.ShapeDtypeStruct((GLA_HEADS, N_TOK, GLA_DV), F32),
                   jax.ShapeDtypeStruct((BATCH, n_layers_out, 2, GLA_HEADS, GLA_DK, GLA_DV), F32)],
        scratch_shapes=[pltpu.VMEM((2, GLA_DV, GLA_DK), F32)],
        compiler_params=_params(("arbitrary", "arbitrary")),
        name="gla_scan",
    )(*args)


def _fnet_in_kernel(xc_ref, xl_ref, mod_ref, ng_ref, w32_ref, cs_ref, c_ref, s_ref, uc_ref, us_ref,
                    w_ref):
    i = pl.program_id(0)

    @pl.when(i == 0)
    def _():
        w_ref[...] = w32_ref[...].astype(BF16)

    x = _read_rows(i, xc_ref, xl_ref)
    h = _rms(x, ng_ref[...]) * (1.0 + _mod_part(mod_ref, 1)) + _mod_part(mod_ref, 0)
    u = _dot(h.astype(BF16), w_ref[...]).astype(BF16)
    for g in range(FNET_GROUPS):
        cols = slice(g * FNET_GW, (g + 1) * FNET_GW)
        t = _dot(u[:, cols], cs_ref[...])
        uc_ref[:, cols] = t[:, 0:FNET_GW].astype(BF16)
        us_ref[:, cols] = t[:, FNET_GW:2 * FNET_GW].astype(BF16)

    @pl.when(i < N_CTX_BLK)
    def _():
        for s in range(TM // SEQ):
            rows = slice(s * SEQ, (s + 1) * SEQ)
            f = _dot(c_ref[...], uc_ref[rows, :]) - _dot(s_ref[...], us_ref[rows, :])
            uc_ref[rows, :] = (f * SEQ ** -0.5).astype(BF16)


def _fnet_in(xc, xl, mods, layer, norm_g, w_in, cs_chan, c_tab, s_tab):
    row = pl.BlockSpec((TM, D_MODEL), lambda i: (i, 0))
    return pl.pallas_call(
        _fnet_in_kernel,
        grid=(N_TOK // TM,),
        in_specs=[_ctx_rows(), _lat_rows(), _mod_spec(layer), _layer_spec((1, D_MODEL), layer),
                  _layer_spec((D_MODEL, D_MODEL), layer // 2), _const_spec((FNET_GW, 2 * FNET_GW)),
                  _const_spec((SEQ, SEQ)), _const_spec((SEQ, SEQ))],
        out_specs=[row, row],
        out_shape=[jax.ShapeDtypeStruct((N_TOK, D_MODEL), BF16)] * 2,
        scratch_shapes=[pltpu.VMEM((D_MODEL, D_MODEL), BF16)],
        compiler_params=_params(("arbitrary",)),
        name="fnet_in",
    )(xc, xl, mods, norm_g.reshape(DEPTH, 1, D_MODEL), w_in, cs_chan, c_tab, s_tab)


HALF_SEQ = DEC_SEQ // 2


def _dft_lat_kernel(uc_ref, us_ref, c_ref, s_ref, tw_ref, f_ref, ec_scr, es_scr, oc_scr, os_scr):
    @pl.when(pl.program_id(1) == 0)
    def _():
        lo, hi = slice(0, HALF_SEQ), slice(HALF_SEQ, DEC_SEQ)
        c_lo, c_hi = uc_ref[lo, :].astype(F32), uc_ref[hi, :].astype(F32)
        s_lo, s_hi = us_ref[lo, :].astype(F32), us_ref[hi, :].astype(F32)
        ec_scr[...] = (c_lo + c_hi).astype(BF16)
        es_scr[...] = (s_lo + s_hi).astype(BF16)
        dc, ds = c_lo - c_hi, s_lo - s_hi
        cos_l, sin_l = tw_ref[:, 0:1], tw_ref[:, 1:2]
        oc_scr[...] = (cos_l * dc - sin_l * ds).astype(BF16)
        os_scr[...] = (sin_l * dc + cos_l * ds).astype(BF16)

    scale = DEC_SEQ ** -0.5
    even = (_dot(c_ref[...], ec_scr[...]) - _dot(s_ref[...], es_scr[...])) * scale
    odd = (_dot(c_ref[...], oc_scr[...]) - _dot(s_ref[...], os_scr[...])) * scale
    for t in range(D_MODEL // LANES):
        cols = slice(t * LANES, (t + 1) * LANES)
        f_ref[t, pl.ds(0, TM, stride=2), :] = even[:, cols]
        f_ref[t, pl.ds(1, TM, stride=2), :] = odd[:, cols]


def _fnet_dft_lat(uc, us, c_half, s_half, twiddle):
    n_ctx_blk = N_CTX_TOK // DEC_SEQ
    m_tiles = HALF_SEQ // TM
    seq_blk = pl.BlockSpec((DEC_SEQ, D_MODEL), lambda b, m: (n_ctx_blk + b, 0))
    tab = pl.BlockSpec((TM, HALF_SEQ), lambda b, m: (m, 0))
    return pl.pallas_call(
        _dft_lat_kernel,
        grid=(DEC_BATCH, m_tiles),
        in_specs=[seq_blk, seq_blk, tab, tab, _const_spec((HALF_SEQ, 2))],
        out_specs=pl.BlockSpec((D_MODEL // LANES, 2 * TM, LANES),
                               lambda b, m: (0, b * m_tiles + m, 0)),
        out_shape=jax.ShapeDtypeStruct((D_MODEL // LANES, N_LAT_TOK, LANES), F32),
        scratch_shapes=[pltpu.VMEM((HALF_SEQ, D_MODEL), BF16)] * 4,
        compiler_params=_params(("arbitrary", "arbitrary")),
        name="fnet_dft_lat",
    )(uc, us, c_half, s_half, twiddle)


def _cast_kernel(w_ref, o_ref):
    o_ref[...] = w_ref[...].astype(BF16)


def _ffn_up_weights(w_up):
    n_f = FFN_DIM // TF
    return pl.pallas_call(
        _cast_kernel,
        grid=(DEPTH, n_f, 2),
        in_specs=[pl.BlockSpec((None, D_MODEL, TF), lambda l, t, part: (l, 0, part * n_f + t))],
        out_specs=pl.BlockSpec((None, D_MODEL, TF), lambda l, t, part: (l, 0, 2 * t + part)),
        out_shape=jax.ShapeDtypeStruct((DEPTH, D_MODEL, 2 * FFN_DIM), BF16),
        compiler_params=_params(("arbitrary",) * 3),
        name="ffn_up_cast",
    )(w_up)


def _ffn_down_weights(w_down):
    return pl.pallas_call(
        _cast_kernel,
        grid=(DEPTH, FFN_DIM // TF),
        in_specs=[pl.BlockSpec((None, TF, D_MODEL), lambda l, t: (l, t, 0))],
        out_specs=pl.BlockSpec((None, TF, D_MODEL), lambda l, t: (l, t, 0)),
        out_shape=jax.ShapeDtypeStruct((DEPTH, FFN_DIM, D_MODEL), BF16),
        compiler_params=_params(("arbitrary",) * 2),
        name="ffn_down_cast",
    )(w_down)


def _mix_ffn_kernel(is_gla, is_last, *refs):
    xc_ref, xl_ref = refs[:2]
    if is_gla:
        o_ref, r_ref, gn_ref = refs[2:5]
        refs = refs[5:]
    else:
        fc_ref, fl_ref = refs[2:4]
        refs = refs[4:]
    (wo32_ref, mod_ref, ng_ref, wu_ref, cw_ref, cb_ref, wd_ref, fg_ref,
     outc_ref, outl_ref, x_scr, h_scr, acc_scr, out_scr, wo_ref) = refs
    i = pl.program_id(0)
    j = pl.program_id(1)
    n_f = FFN_DIM // TF

    @pl.when((i == 0) & (j == 0))
    def _():
        wo_ref[...] = wo32_ref[...].astype(BF16)

    period = jnp.where(i < N_CTX_BLK, SEQ, GRID_W)
    pos = lax.broadcasted_iota(jnp.int32, (HALF, 1), 0) & (period - 1)
    has_prev = (pos != 0).astype(F32)
    has_next = (pos != period - 1).astype(F32)

    def mixer_out(rows):
        if is_gla:
            parts = []
            for hd in range(GLA_HEADS):
                cols = slice(hd * GLA_DV, (hd + 1) * GLA_DV)
                y = _rms(o_ref[hd, rows, :], gn_ref[...]) * _silu(r_ref[rows, cols])
                parts.append(y.astype(BF16))
            y = jnp.concatenate(parts, axis=1)
        else:
            f_lat = jnp.concatenate([fl_ref[t, rows, :] for t in range(D_MODEL // LANES)], axis=1)
            y = jnp.where(i < N_CTX_BLK, fc_ref[rows, :], f_lat.astype(BF16))
        x_new = _read_rows(i, xc_ref, xl_ref, rows) + _mod_part(mod_ref, 2) * _dot(y, wo_ref[...])
        x_scr[rows, :] = x_new
        h = _rms(x_new, ng_ref[...]) * (1.0 + _mod_part(mod_ref, 4)) + _mod_part(mod_ref, 3)
        hb = h.astype(BF16)
        h_scr[rows, :] = hb
        return hb

    def ffn_tile(hb):
        ag = _dot(hb, wu_ref[...])
        a = ag[:, 0:TF]
        g = ag[:, TF:2 * TF]
        a_prev = pltpu.roll(a, 1, 0) * has_prev
        a_next = pltpu.roll(a, HALF - 1, 0) * has_next
        conv = a_prev * cw_ref[0:1, :] + a * cw_ref[1:2, :] + a_next * cw_ref[2:3, :] + cb_ref[...]
        return _dot((_silu(conv) * g).astype(BF16), wd_ref[...])

    def step(first, last):
        for s in range(TM // HALF):
            rows = slice(s * HALF, (s + 1) * HALF)
            hb = mixer_out(rows) if first else h_scr[rows, :]
            d = ffn_tile(hb)
            if not first:
                d = acc_scr[rows, :] + d
            if last:
                out = x_scr[rows, :] + _mod_part(mod_ref, 5) * d
                if is_last:
                    out = _rms(out, fg_ref[...])
                out_scr[rows, :] = out
            else:
                acc_scr[rows, :] = d

    pl.when(j == 0)(lambda: step(True, n_f == 1))
    if n_f > 2:
        pl.when((j > 0) & (j < n_f - 1))(lambda: step(False, False))
    if n_f > 1:
        pl.when(j == n_f - 1)(lambda: step(False, True))

    @pl.when(j == n_f - 1)
    def _():
        _write_rows(i, outc_ref, outl_ref, out_scr[...])


def _mix_ffn(is_gla, is_last, xc, xl, mix_in, w_o, mods, layer, norm_g, w_up, conv_w, conv_b,
             w_down, final_g):
    n_f = FFN_DIM // TF
    row = pl.BlockSpec((TM, D_MODEL), lambda i, j: (i, 0))
    if is_gla:
        mix_specs = [pl.BlockSpec((GLA_HEADS, TM, GLA_DV), lambda i, j: (0, i, 0)), row,
                     _layer_spec((1, GLA_DV), layer // 2)]
    else:
        mix_specs = [_ctx_rows(),
                     pl.BlockSpec((D_MODEL // LANES, TM, LANES),
                                  lambda i, j: (0, jnp.maximum(i - N_CTX_BLK, 0), 0))]
    in_specs = [_ctx_rows(), _lat_rows()] + mix_specs + [
        _layer_spec((D_MODEL, D_MODEL), layer // 2), _mod_spec(layer),
        _layer_spec((1, D_MODEL), layer),
        pl.BlockSpec((None, D_MODEL, 2 * TF), lambda i, j: (layer, 0, j)),
        pl.BlockSpec((None, 3, TF), lambda i, j: (layer, 0, j)),
        pl.BlockSpec((None, 1, TF), lambda i, j: (layer, 0, j)),
        pl.BlockSpec((None, TF, D_MODEL), lambda i, j: (layer, j, 0)),
        _const_spec((1, D_MODEL)),
    ]
    return pl.pallas_call(
        functools.partial(_mix_ffn_kernel, is_gla, is_last),
        grid=(N_TOK // TM, FFN_DIM // TF),
        in_specs=in_specs,
        out_specs=[_ctx_rows(), _lat_rows()],
        out_shape=[jax.ShapeDtypeStruct((N_CTX_TOK, D_MODEL), F32),
                   jax.ShapeDtypeStruct((N_LAT_TOK, D_MODEL), F32)],
        scratch_shapes=[pltpu.VMEM((TM, D_MODEL), F32), pltpu.VMEM((TM, D_MODEL), BF16),
                        pltpu.VMEM((TM, D_MODEL), F32), pltpu.VMEM((TM, D_MODEL), F32),
                        pltpu.VMEM((D_MODEL, D_MODEL), BF16)],
        compiler_params=_params(("arbitrary", "arbitrary")),
        name="gla_out_ffn" if is_gla else "fnet_out_ffn",
    )(xc, xl, *mix_in, w_o, mods, norm_g.reshape(DEPTH, 1, D_MODEL), w_up, conv_w,
      conv_b.reshape(DEPTH, 1, FFN_DIM), w_down, final_g)


def _dft_tables(n, scale):
    idx = np.arange(n, dtype=np.int64)
    ang = (2.0 * np.pi / n) * ((idx[:, None] * idx[None, :]) % n).astype(np.float64)
    return (np.cos(ang) * scale).astype(np.float32), (np.sin(ang) * scale).astype(np.float32)


def _chunk_triangles():
    idx = np.arange(SUPER)
    same = (idx[:, None] // CHUNK) == (idx[None, :] // CHUNK)
    low = same & (idx[None, :] <= idx[:, None])
    up = same & (idx[None, :] >= idx[:, None])
    return low.astype(np.float32), up.astype(np.float32)


def kernel(x_prompt, x_sample, state_gla, c, c_ctx, norm_mix_g, norm_ffn_g, w_mod, b_mod,
           gla_w_in, gla_w_g2, gla_b_g, gla_norm_g, gla_w_o, fnet_w_in, fnet_w_o,
           ffn_w_up, ffn_conv_w, ffn_conv_b, ffn_w_down, final_norm_g):
    xc = x_prompt.reshape(N_CTX_TOK, D_MODEL)
    xl = x_sample.reshape(N_LAT_TOK, D_MODEL)
    c_cols = jnp.concatenate([c_ctx[None, :], c], axis=0)[:, :, None]
    mods = _mods(c_cols, w_mod, b_mod)

    tlow_np, tup_np = _chunk_triangles()
    tlow, tup = jnp.asarray(tlow_np).astype(BF16), jnp.asarray(tup_np).astype(BF16)
    cc, sc = _dft_tables(FNET_GW, FNET_GW ** -0.5)
    cs_chan = jnp.asarray(np.concatenate([cc, sc], axis=1)).astype(BF16)
    c_ctx_tab, s_ctx_tab = (jnp.asarray(t).astype(BF16) for t in _dft_tables(SEQ, 1.0))
    c_lat_tab, s_lat_tab = (jnp.asarray(t).astype(BF16) for t in _dft_tables(HALF_SEQ, 1.0))
    phase = (2.0 * np.pi / DEC_SEQ) * np.arange(HALF_SEQ, dtype=np.float64)
    twiddle = jnp.asarray(np.stack([np.cos(phase), np.sin(phase)], axis=1).astype(np.float32))
    final_g = final_norm_g.reshape(1, D_MODEL)

    w_up = _ffn_up_weights(ffn_w_up)
    w_down = _ffn_down_weights(ffn_w_down)
    head_norm_g = gla_norm_g.reshape(N_GLA_LAYERS, 1, GLA_DV)

    states = None
    for layer in range(DEPTH):
        j = layer // 2
        if layer % 2 == 0:
            qf, kf, qb, kb, v, r, oi, df, db = _gla_in(
                xc, xl, mods, layer, norm_mix_g, gla_w_in, gla_w_g2, gla_b_g, tlow, tup)
            o, states = _gla_scan(qf, kf, qb, kb, v, oi, df, db, state_gla, j, states)
            mix_in = (o, r, head_norm_g)
            w_o = gla_w_o
        else:
            uc, us = _fnet_in(xc, xl, mods, layer, norm_mix_g, fnet_w_in, cs_chan,
                              c_ctx_tab, s_ctx_tab)
            f_lat = _fnet_dft_lat(uc, us, c_lat_tab, s_lat_tab, twiddle)
            mix_in = (uc, f_lat)
            w_o = fnet_w_o
        xc, xl = _mix_ffn(layer % 2 == 0, layer == DEPTH - 1, xc, xl, mix_in, w_o, mods, layer,
                          norm_ffn_g, w_up, ffn_conv_w, ffn_conv_b, w_down, final_g)

    return (xc.reshape(BATCH, SEQ, D_MODEL), xl.reshape(DEC_BATCH, DEC_SEQ, D_MODEL), states)
```

```python
import functools

import numpy as np
import jax
import jax.numpy as jnp
from jax import lax
from jax.experimental import pallas as pl
from jax.experimental.pallas import tpu as pltpu

F32 = jnp.float32
BF16 = jnp.bfloat16

D_MODEL = 1024
BATCH = 16
SEQ = 256
DEPTH = 4
DEC_BATCH = 2
DEC_SEQ = 2048
GRID_W = 64
N_GLA_LAYERS = 2
GLA_HEADS = 4
GLA_QK = D_MODEL // 2
GLA_V = D_MODEL
GLA_DK = GLA_QK // GLA_HEADS
GLA_DV = GLA_V // GLA_HEADS
GATE_RANK = 16
GATE_TAU = 16.0
GLA_IN = 2 * GLA_QK + 2 * GLA_V + 2 * GATE_RANK
FNET_GROUPS = 4
FNET_GW = D_MODEL // FNET_GROUPS
FFN_DIM = 2816
EPS = 1e-6

N_CTX_TOK = BATCH * SEQ
N_LAT_TOK = DEC_BATCH * DEC_SEQ
N_TOK = N_CTX_TOK + N_LAT_TOK
N_GROUPS = 1 + DEC_BATCH

LANES = 128
SUBLANES = 8
CHUNK = 64
SUPER = 256
TM = 512
N_CTX_BLK = N_CTX_TOK // TM
SCAN_ROWS = 2048
N_CTX_SCAN = N_CTX_TOK // SCAN_ROWS
SCAN_STEPS = 16
HALF = 256
TF = 1408
GLA_IN_PAD = 3328
GLR_COL = 2 * GLA_QK + 2 * GLA_V
VMEM_LIMIT = 56 * 1024 * 1024
MOD_ROWS = 512
MOD_TILES = 8


def _dot(a, b):
    return jnp.dot(a, b, preferred_element_type=F32)


def _dot_nt(a, b):
    return lax.dot_general(a, b, (((1,), (1,)), ((), ())), preferred_element_type=F32)


def _dot_tn(a, b):
    return lax.dot_general(a, b, (((0,), (0,)), ((), ())), preferred_element_type=F32)


def _silu(x):
    return x / (1.0 + jnp.exp(-x))


def _log_sigmoid(x):
    return jnp.minimum(x, 0.0) - jnp.log(1.0 + jnp.exp(-jnp.abs(x)))


def _rms(x, g):
    return x * lax.rsqrt(jnp.mean(x * x, axis=-1, keepdims=True) + EPS) * g


def _group_of_block(i):
    return jnp.where(i < N_CTX_BLK, 0, 1 + (i - N_CTX_BLK) // (DEC_SEQ // TM))


def _params(sem):
    return pltpu.CompilerParams(dimension_semantics=sem, vmem_limit_bytes=VMEM_LIMIT)


def _const_spec(shape):
    nd = len(shape)
    return pl.BlockSpec(shape, lambda *_: (0,) * nd, pipeline_mode=pl.Buffered(1))


def _layer_spec(shape, layer):
    nd = len(shape)
    return pl.BlockSpec((None,) + tuple(shape), lambda *_: (layer,) + (0,) * nd,
                        pipeline_mode=pl.Buffered(1))


def _mod_spec(layer):
    return pl.BlockSpec((None, None, 1, 6 * D_MODEL),
                        lambda i, *_: (layer, _group_of_block(i), 0, 0))


def _mod_part(mod_ref, k):
    return mod_ref[:, k * D_MODEL:(k + 1) * D_MODEL]


def _ctx_rows(width=D_MODEL):
    return pl.BlockSpec((TM, width), lambda i, *_: (jnp.minimum(i, N_CTX_BLK - 1), 0))


def _lat_rows(width=D_MODEL):
    return pl.BlockSpec((TM, width), lambda i, *_: (jnp.maximum(i - N_CTX_BLK, 0), 0))


def _read_rows(i, ctx_ref, lat_ref, rows=slice(None)):
    return jnp.where(i < N_CTX_BLK, ctx_ref[rows, :], lat_ref[rows, :])


def _write_rows(i, ctx_ref, lat_ref, value):
    @pl.when(i < N_CTX_BLK)
    def _():
        ctx_ref[...] = value

    @pl.when(i >= N_CTX_BLK)
    def _():
        lat_ref[...] = value


def _mods_kernel(c_ref, w_ref, b_ref, o_ref, sb_scr):
    k = pl.program_id(1)

    @pl.when((pl.program_id(0) == 0) & (k == 0))
    def _():
        for v in range(N_GROUPS):
            sb_scr[v] = jnp.broadcast_to(_silu(c_ref[v]), (D_MODEL, LANES))

    @pl.when(k == 0)
    def _():
        for v in range(N_GROUPS):
            o_ref[v] = b_ref[...]

    row_base = k * MOD_ROWS
    zero = jnp.zeros((SUBLANES, LANES), F32)
    for grp in range(6 * D_MODEL // (MOD_TILES * LANES)):
        col0 = grp * MOD_TILES * LANES

        def slab(r, accs, col0=col0):
            r0 = pl.multiple_of(r * SUBLANES, SUBLANES)
            s = [sb_scr[v, pl.ds(row_base + r0, SUBLANES), :] for v in range(N_GROUPS)]
            out = []
            for t in range(MOD_TILES):
                w = w_ref[pl.ds(r0, SUBLANES), col0 + t * LANES:col0 + (t + 1) * LANES]
                out += [accs[t * N_GROUPS + v] + w * s[v] for v in range(N_GROUPS)]
            return tuple(out)

        accs = lax.fori_loop(0, MOD_ROWS // SUBLANES, slab, (zero,) * (MOD_TILES * N_GROUPS),
                             unroll=8)
        for t in range(MOD_TILES):
            cols = slice(col0 + t * LANES, col0 + (t + 1) * LANES)
            for v in range(N_GROUPS):
                o_ref[v, :, cols] += jnp.sum(accs[t * N_GROUPS + v], axis=0, keepdims=True)


def _mods(c_cols, w_mod, b_mod):
    n_out = 6 * D_MODEL
    return pl.pallas_call(
        _mods_kernel,
        grid=(DEPTH, D_MODEL // MOD_ROWS),
        in_specs=[
            pl.BlockSpec((N_GROUPS, D_MODEL, 1), lambda l, k: (0, 0, 0)),
            pl.BlockSpec((None, MOD_ROWS, n_out), lambda l, k: (l, k, 0)),
            pl.BlockSpec((None, 1, n_out), lambda l, k: (l, 0, 0)),
        ],
        out_specs=pl.BlockSpec((None, N_GROUPS, 1, n_out), lambda l, k: (l, 0, 0, 0)),
        out_shape=jax.ShapeDtypeStruct((DEPTH, N_GROUPS, 1, n_out), F32),
        scratch_shapes=[pltpu.VMEM((N_GROUPS, D_MODEL, LANES), F32)],
        compiler_params=_params(("arbitrary", "arbitrary")),
        name="adaln_mods",
    )(c_cols, w_mod, b_mod.reshape(DEPTH, 1, n_out))


def _cumsum2(t, g):
    g1 = g.astype(BF16)
    g2 = (g - g1.astype(F32)).astype(BF16)
    return _dot(t, g1) + _dot(t, g2)


def _rows_bcast(b, first_row):
    n = b.shape[0] // CHUNK
    return jnp.concatenate(
        [jnp.broadcast_to(b[c * CHUNK + first_row:c * CHUNK + first_row + 1, :], (CHUNK, b.shape[1]))
         for c in range(n)], axis=0)


def _store_heads(ref, rows, value):
    width = ref.shape[-1]
    for hd in range(ref.shape[0]):
        ref[hd, rows, :] = value[:, hd * width:(hd + 1) * width]


def _gla_in_kernel(xc_ref, xl_ref, mod_ref, ng_ref, w32_ref, wg32_ref, bg_ref, tlow_ref, tup_ref,
                   qf_ref, kf_ref, qb_ref, kb_ref, v_ref, r_ref, oi_ref, df_ref, db_ref,
                   w_ref, wg2_ref):
    i = pl.program_id(0)

    @pl.when(i == 0)
    def _():
        w_ref[:, 0:GLR_COL] = w32_ref[:, 0:GLR_COL].astype(BF16)
        w_ref[:, GLR_COL:GLA_IN_PAD] = jnp.zeros((D_MODEL, GLA_IN_PAD - GLR_COL), BF16)
        w_ref[:, GLR_COL:GLA_IN] = w32_ref[:, GLR_COL:GLA_IN].astype(BF16)
        wg2_ref[...] = jnp.zeros_like(wg2_ref)
        for z in range(2):
            wg2_ref[z, z * GATE_RANK:(z + 1) * GATE_RANK, :] = wg32_ref[z].astype(BF16)

    ri = lax.broadcasted_iota(jnp.int32, (SUPER, SUPER), 0)
    ci = lax.broadcasted_iota(jnp.int32, (SUPER, SUPER), 1)
    same = (ri & -CHUNK) == (ci & -CHUNK)
    low = same & (ci <= ri)
    up = same & (ci >= ri)
    cpg = SUPER // CHUNK
    inv_tau = 1.0 / GATE_TAU

    def project(s, res):
        rows = slice(s * SUPER, (s + 1) * SUPER)
        x = _read_rows(i, xc_ref, xl_ref, rows)
        h = _rms(x, ng_ref[...]) * (1.0 + _mod_part(mod_ref, 1)) + _mod_part(mod_ref, 0)
        hb = h.astype(BF16)
        yield
        res["q"] = _dot(hb, w_ref[:, 0:GLA_QK]) * (GLA_DK ** -0.5)
        yield
        res["k"] = _dot(hb, w_ref[:, GLA_QK:2 * GLA_QK])
        yield
        res["v"] = []
        half_v = GLA_V // 2
        for part in range(2):
            c0 = 2 * GLA_QK + part * half_v
            vp = _dot(hb, w_ref[:, c0:c0 + half_v]).astype(BF16)
            res["v"].append(vp)
            for hh in range(GLA_HEADS // 2):
                v_ref[part * (GLA_HEADS // 2) + hh, rows, :] = vp[:, hh * GLA_DV:(hh + 1) * GLA_DV]
            yield
        for part in range(2):
            c0 = 2 * GLA_QK + GLA_V + part * half_v
            r_ref[rows, part * half_v:(part + 1) * half_v] = _dot(hb, w_ref[:, c0:c0 + half_v])
            yield
        res["glr"] = _dot(hb, w_ref[:, GLR_COL:GLA_IN_PAD]).astype(BF16)

    def local(s, res):
        rows = slice(s * SUPER, (s + 1) * SUPER)
        qs, ks, vb, glr = res["q"], res["k"], res["v"], res["glr"]
        gf = _log_sigmoid(_dot(glr, wg2_ref[0]) + bg_ref[0:1, :]) * inv_tau
        yield
        gb = _log_sigmoid(_dot(glr, wg2_ref[1]) + bg_ref[1:2, :]) * inv_tau
        yield
        b = _cumsum2(tlow_ref[...], gf)
        bmid = _rows_bcast(b, CHUNK // 2 - 1)
        blast = _rows_bcast(b, CHUNK - 1)
        qtf = (qs * jnp.exp(b - bmid)).astype(BF16)
        ktf = (ks * jnp.exp(bmid - b)).astype(BF16)
        yield
        _store_heads(qf_ref, rows, (qs * jnp.exp(b)).astype(BF16))
        _store_heads(kf_ref, rows, (ks * jnp.exp(blast - b)).astype(BF16))
        for c in range(cpg):
            df_ref[s * cpg + c:s * cpg + c + 1, :] = jnp.exp(b[c * CHUNK + CHUNK - 1:(c + 1) * CHUNK, :])
        yield
        b = _cumsum2(tup_ref[...], gb)
        bmid = _rows_bcast(b, CHUNK // 2)
        blast = _rows_bcast(b, 0)
        qtb = (qs * jnp.exp(b - bmid)).astype(BF16)
        ktb = (ks * jnp.exp(bmid - b)).astype(BF16)
        yield
        _store_heads(qb_ref, rows, (qs * jnp.exp(b)).astype(BF16))
        _store_heads(kb_ref, rows, (ks * jnp.exp(blast - b)).astype(BF16))
        for c in range(cpg):
            db_ref[s * cpg + c:s * cpg + c + 1, :] = jnp.exp(b[c * CHUNK:c * CHUNK + 1, :])
        yield
        for hd in range(GLA_HEADS):
            kc = slice(hd * GLA_DK, (hd + 1) * GLA_DK)
            af = _dot_nt(qtf[:, kc], ktf[:, kc])
            ab = _dot_nt(qtb[:, kc], ktb[:, kc])
            a = (jnp.where(low, af, 0.0) + jnp.where(up, ab, 0.0)).astype(BF16)
            per_part = GLA_HEADS // 2
            vh = vb[hd // per_part][:, (hd % per_part) * GLA_DV:(hd % per_part + 1) * GLA_DV]
            oi_ref[hd, rows, :] = _dot(a, vh)
            if hd % per_part == per_part - 1 and hd < GLA_HEADS - 1:
                yield

    n_groups = TM // SUPER
    results = [dict() for _ in range(n_groups)]
    for s in range(n_groups + 1):
        stages = []
        if s < n_groups:
            stages.append(project(s, results[s]))
        if s > 0:
            stages.append(local(s - 1, results[s - 1]))
        while stages:
            stages = [g for g in stages if next(g, "done") != "done"]


def _gla_in(xc, xl, mods, layer, norm_g, w_in, w_g2, b_g, tlow, tup):
    gla_layer = layer // 2
    cpb = TM // CHUNK
    row = lambda w: pl.BlockSpec((TM, w), lambda i: (i, 0))
    heads = lambda w: pl.BlockSpec((GLA_HEADS, TM, w), lambda i: (0, i, 0))
    outs = [
        jax.ShapeDtypeStruct((GLA_HEADS, N_TOK, GLA_DK), BF16),
        jax.ShapeDtypeStruct((GLA_HEADS, N_TOK, GLA_DK), BF16),
        jax.ShapeDtypeStruct((GLA_HEADS, N_TOK, GLA_DK), BF16),
        jax.ShapeDtypeStruct((GLA_HEADS, N_TOK, GLA_DK), BF16),
        jax.ShapeDtypeStruct((GLA_HEADS, N_TOK, GLA_DV), BF16),
        jax.ShapeDtypeStruct((N_TOK, GLA_V), F32),
        jax.ShapeDtypeStruct((GLA_HEADS, N_TOK, GLA_DV), F32),
        jax.ShapeDtypeStruct((N_TOK // CHUNK, GLA_QK), F32),
        jax.ShapeDtypeStruct((N_TOK // CHUNK, GLA_QK), F32),
    ]
    return pl.pallas_call(
        _gla_in_kernel,
        grid=(N_TOK // TM,),
        in_specs=[
            _ctx_rows(), _lat_rows(), _mod_spec(layer), _layer_spec((1, D_MODEL), layer),
            _layer_spec((D_MODEL, GLA_IN), gla_layer),
            _layer_spec((2, GATE_RANK, GLA_QK), gla_layer), _layer_spec((2, GLA_QK), gla_layer),
            _const_spec((SUPER, SUPER)), _const_spec((SUPER, SUPER)),
        ],
        out_specs=[heads(GLA_DK)] * 4 + [heads(GLA_DV), row(GLA_V), heads(GLA_DV)]
        + [pl.BlockSpec((cpb, GLA_QK), lambda i: (i, 0))] * 2,
        out_shape=outs,
        scratch_shapes=[pltpu.VMEM((D_MODEL, GLA_IN_PAD), BF16),
                        pltpu.VMEM((2, GLA_IN_PAD - GLR_COL, GLA_QK), BF16)],
        compiler_params=_params(("arbitrary",)),
        name="gla_in",
    )(xc, xl, mods, norm_g.reshape(DEPTH, 1, D_MODEL), w_in, w_g2, b_g, tlow, tup)


def _chunk_rows(chunks):
    return [pl.multiple_of(ch * CHUNK, CHUNK) for ch in chunks]


def _scan_updates(k_ref, v_ref, chunks):
    return [_dot_tn(v_ref[pl.ds(r0, CHUNK), :], k_ref[pl.ds(r0, CHUNK), :])
            for r0 in _chunk_rows(chunks)]


def _scan_chain(q_ref, d_ref, chunks, updates, st):
    reads = []
    for ch, r0, u in zip(chunks, _chunk_rows(chunks), updates):
        reads.append((r0, _dot_nt(q_ref[pl.ds(r0, CHUNK), :], st.astype(BF16))))
        st = st * d_ref[pl.ds(ch, 1), :] + u
    return st, reads


def _gla_scan_kernel(has_prev, *refs):
    (qf_ref, kf_ref, qb_ref, kb_ref, v_ref, oi_ref, df_ref, db_ref, s0_ref) = refs[:9]
    refs = refs[9:]
    if has_prev:
        prev_ref, refs = refs[0], refs[1:]
    o_ref, st_ref, s_scr = refs
    i = pl.program_id(1)
    o_ref[...] = oi_ref[...]

    def accumulate(reads):
        for r0, val in reads:
            o_ref[pl.ds(r0, CHUNK), :] += val

    @pl.when(i < N_CTX_SCAN)
    def _():
        cps = SEQ // CHUNK
        seq_per_trip = SCAN_STEPS // cps
        layer_slot = N_GLA_LAYERS - 1 if has_prev else 0
        if has_prev:
            for l in range(N_GLA_LAYERS - 1):
                st_ref[:, l] = prev_ref[:, l]

        def trip(t, carry):
            seqs = [t * seq_per_trip + u for u in range(seq_per_trip)]
            fwd = [[s * cps + c for c in range(cps)] for s in seqs]
            bwd = [[s * cps + (cps - 1 - c) for c in range(cps)] for s in seqs]
            uf = [_scan_updates(kf_ref, v_ref, ch) for ch in fwd]
            ub = [_scan_updates(kb_ref, v_ref, ch) for ch in bwd]
            zero = jnp.zeros((GLA_DV, GLA_DK), F32)
            for u, s in enumerate(seqs):
                sf, rf = _scan_chain(qf_ref, df_ref, fwd[u], uf[u], zero)
                sb, rb = _scan_chain(qb_ref, db_ref, bwd[u], ub[u], zero)
                accumulate(rf + rb)
                st_ref[s, layer_slot, 0] = sf.T
                st_ref[s, layer_slot, 1] = sb.T
            return carry

        lax.fori_loop(0, SCAN_ROWS // SEQ // seq_per_trip, trip, 0)

    @pl.when(i >= N_CTX_SCAN)
    def _():
        n_chunks = SCAN_ROWS // CHUNK
        s_scr[0] = s0_ref[0].T
        s_scr[1] = s0_ref[1].T

        def trip(t, carry):
            first = t * SCAN_STEPS
            fwd = [first + c for c in range(SCAN_STEPS)]
            bwd = [n_chunks - 1 - first - c for c in range(SCAN_STEPS)]
            uf = _scan_updates(kf_ref, v_ref, fwd)
            ub = _scan_updates(kb_ref, v_ref, bwd)
            sf, rf = _scan_chain(qf_ref, df_ref, fwd, uf, s_scr[0])
            sb, rb = _scan_chain(qb_ref, db_ref, bwd, ub, s_scr[1])
            accumulate(rf + rb)
            s_scr[0] = sf
            s_scr[1] = sb
            return carry

        lax.fori_loop(0, n_chunks // SCAN_STEPS, trip, 0)


def _gla_scan(qf, kf, qb, kb, v, oi, df, db, state_gla, gla_layer, prev_states):
    has_prev = prev_states is not None
    n_layers_out = gla_layer + 1
    cpb = SCAN_ROWS // CHUNK
    seq_per_blk = SCAN_ROWS // SEQ
    qk_spec = pl.BlockSpec((None, SCAN_ROWS, GLA_DK), lambda h, i: (h, i, 0))
    v_spec = pl.BlockSpec((None, SCAN_ROWS, GLA_DV), lambda h, i: (h, i, 0))
    d_spec = pl.BlockSpec((cpb, GLA_DK), lambda h, i: (i, h))
    ctx_blk = lambda h, i: jnp.minimum(i, N_CTX_SCAN - 1)
    in_specs = [qk_spec, qk_spec, qk_spec, qk_spec, v_spec, v_spec, d_spec, d_spec,
                pl.BlockSpec((None, None, 2, None, GLA_DK, GLA_DV),
                             lambda h, i: (jnp.maximum(i - N_CTX_SCAN, 0), gla_layer, 0, h, 0, 0))]
    args = [qf, kf, qb, kb, v, oi, df, db, state_gla]
    if has_prev:
        in_specs.append(pl.BlockSpec((seq_per_blk, gla_layer, 2, None, GLA_DK, GLA_DV),
                                     lambda h, i: (ctx_blk(h, i), 0, 0, h, 0, 0)))
        args.append(prev_states)
    return pl.pallas_call(
        functools.partial(_gla_scan_kernel, has_prev),
        grid=(GLA_HEADS, N_TOK // SCAN_ROWS),
        in_specs=in_specs,
        out_specs=[
            v_spec,
            pl.BlockSpec((seq_per_blk, n_layers_out, 2, None, GLA_DK, GLA_DV),
                         lambda h, i: (ctx_blk(h, i), 0, 0, h, 0, 0)),
        ],
        out_shape=[jax.ShapeDtypeStruct((GLA_HEADS, N_TOK, GLA_DV), F32),
                   jax.ShapeDtypeStruct((BATCH, n_layers_out, 2, GLA_HEADS, GLA_DK, GLA_DV), F32)],
        scratch_shapes=[pltpu.VMEM((2, GLA_DV, GLA_DK), F32)],
        compiler_params=_params(("arbitrary", "arbitrary")),
        name="gla_scan",
    )(*args)


def _fnet_in_kernel(xc_ref, xl_ref, mod_ref, ng_ref, w32_ref, cs_ref, c_ref, s_ref, uc_ref, us_ref,
                    w_ref):
    i = pl.program_id(0)

    @pl.when(i == 0)
    def _():
        w_ref[...] = w32_ref[...].astype(BF16)

    x = _read_rows(i, xc_ref, xl_ref)
    h = _rms(x, ng_ref[...]) * (1.0 + _mod_part(mod_ref, 1)) + _mod_part(mod_ref, 0)
    u = _dot(h.astype(BF16), w_ref[...]).astype(BF16)
    for g in range(FNET_GROUPS):
        cols = slice(g * FNET_GW, (g + 1) * FNET_GW)
        t = _dot(u[:, cols], cs_ref[...])
        uc_ref[:, cols] = t[:, 0:FNET_GW].astype(BF16)
        us_ref[:, cols] = t[:, FNET_GW:2 * FNET_GW].astype(BF16)

    @pl.when(i < N_CTX_BLK)
    def _():
        for s in range(TM // SEQ):
            rows = slice(s * SEQ, (s + 1) * SEQ)
            f = _dot(c_ref[...], uc_ref[rows, :]) - _dot(s_ref[...], us_ref[rows, :])
            uc_ref[rows, :] = (f * SEQ ** -0.5).astype(BF16)


def _fnet_in(xc, xl, mods, layer, norm_g, w_in, cs_chan, c_tab, s_tab):
    row = pl.BlockSpec((TM, D_MODEL), lambda i: (i, 0))
    return pl.pallas_call(
        _fnet_in_kernel,
        grid=(N_TOK // TM,),
        in_specs=[_ctx_rows(), _lat_rows(), _mod_spec(layer), _layer_spec((1, D_MODEL), layer),
                  _layer_spec((D_MODEL, D_MODEL), layer // 2), _const_spec((FNET_GW, 2 * FNET_GW)),
                  _const_spec((SEQ, SEQ)), _const_spec((SEQ, SEQ))],
        out_specs=[row, row],
        out_shape=[jax.ShapeDtypeStruct((N_TOK, D_MODEL), BF16)] * 2,
        scratch_shapes=[pltpu.VMEM((D_MODEL, D_MODEL), BF16)],
        compiler_params=_params(("arbitrary",)),
        name="fnet_in",
    )(xc, xl, mods, norm_g.reshape(DEPTH, 1, D_MODEL), w_in, cs_chan, c_tab, s_tab)


HALF_SEQ = DEC_SEQ // 2


def _dft_lat_kernel(uc_ref, us_ref, c_ref, s_ref, tw_ref, f_ref, ec_scr, es_scr, oc_scr, os_scr):
    @pl.when(pl.program_id(1) == 0)
    def _():
        lo, hi = slice(0, HALF_SEQ), slice(HALF_SEQ, DEC_SEQ)
        c_lo, c_hi = uc_ref[lo, :].astype(F32), uc_ref[hi, :].astype(F32)
        s_lo, s_hi = us_ref[lo, :].astype(F32), us_ref[hi, :].astype(F32)
        ec_scr[...] = (c_lo + c_hi).astype(BF16)
        es_scr[...] = (s_lo + s_hi).astype(BF16)
        dc, ds = c_lo - c_hi, s_lo - s_hi
        cos_l, sin_l = tw_ref[:, 0:1], tw_ref[:, 1:2]
        oc_scr[...] = (cos_l * dc - sin_l * ds).astype(BF16)
        os_scr[...] = (sin_l * dc + cos_l * ds).astype(BF16)

    scale = DEC_SEQ ** -0.5
    even = (_dot(c_ref[...], ec_scr[...]) - _dot(s_ref[...], es_scr[...])) * scale
    odd = (_dot(c_ref[...], oc_scr[...]) - _dot(s_ref[...], os_scr[...])) * scale
    for t in range(D_MODEL // LANES):
        cols = slice(t * LANES, (t + 1) * LANES)
        f_ref[t, pl.ds(0, TM, stride=2), :] = even[:, cols]
        f_ref[t, pl.ds(1, TM, stride=2), :] = odd[:, cols]


def _fnet_dft_lat(uc, us, c_half, s_half, twiddle):
    n_ctx_blk = N_CTX_TOK // DEC_SEQ
    m_tiles = HALF_SEQ // TM
    seq_blk = pl.BlockSpec((DEC_SEQ, D_MODEL), lambda b, m: (n_ctx_blk + b, 0))
    tab = pl.BlockSpec((TM, HALF_SEQ), lambda b, m: (m, 0))
    return pl.pallas_call(
        _dft_lat_kernel,
        grid=(DEC_BATCH, m_tiles),
        in_specs=[seq_blk, seq_blk, tab, tab, _const_spec((HALF_SEQ, 2))],
        out_specs=pl.BlockSpec((D_MODEL // LANES, 2 * TM, LANES),
                               lambda b, m: (0, b * m_tiles + m, 0)),
        out_shape=jax.ShapeDtypeStruct((D_MODEL // LANES, N_LAT_TOK, LANES), F32),
        scratch_shapes=[pltpu.VMEM((HALF_SEQ, D_MODEL), BF16)] * 4,
        compiler_params=_params(("arbitrary", "arbitrary")),
        name="fnet_dft_lat",
    )(uc, us, c_half, s_half, twiddle)


def _cast_kernel(w_ref, o_ref):
    o_ref[...] = w_ref[...].astype(BF16)


def _ffn_up_weights(w_up):
    n_f = FFN_DIM // TF
    return pl.pallas_call(
        _cast_kernel,
        grid=(DEPTH, n_f, 2),
        in_specs=[pl.BlockSpec((None, D_MODEL, TF), lambda l, t, part: (l, 0, part * n_f + t))],
        out_specs=pl.BlockSpec((None, None, D_MODEL, TF), lambda l, t, part: (l, t, 0, part)),
        out_shape=jax.ShapeDtypeStruct((DEPTH, n_f, D_MODEL, 2 * TF), BF16),
        compiler_params=_params(("arbitrary",) * 3),
        name="ffn_up_cast",
    )(w_up)


def _ffn_down_weights(w_down):
    return pl.pallas_call(
        _cast_kernel,
        grid=(DEPTH, FFN_DIM // TF),
        in_specs=[pl.BlockSpec((None, TF, D_MODEL), lambda l, t: (l, t, 0))],
        out_specs=pl.BlockSpec((None, TF, D_MODEL), lambda l, t: (l, t, 0)),
        out_shape=jax.ShapeDtypeStruct((DEPTH, FFN_DIM, D_MODEL), BF16),
        compiler_params=_params(("arbitrary",) * 2),
        name="ffn_down_cast",
    )(w_down)


def _mix_ffn_kernel(is_gla, is_last, *refs):
    xc_ref, xl_ref = refs[:2]
    if is_gla:
        o_ref, r_ref, gn_ref = refs[2:5]
        refs = refs[5:]
    else:
        fc_ref, fl_ref = refs[2:4]
        refs = refs[4:]
    (wo32_ref, mod_ref, ng_ref, wu_ref, cw_ref, cb_ref, wd_ref, fg_ref,
     outc_ref, outl_ref, x_scr, h_scr, acc_scr, out_scr, wo_ref) = refs
    i = pl.program_id(0)
    j = pl.program_id(1)
    n_f = FFN_DIM // TF

    @pl.when((i == 0) & (j == 0))
    def _():
        wo_ref[...] = wo32_ref[...].astype(BF16)

    period = jnp.where(i < N_CTX_BLK, SEQ, GRID_W)
    pos = lax.broadcasted_iota(jnp.int32, (HALF, 1), 0) & (period - 1)
    has_prev = (pos != 0).astype(F32)
    has_next = (pos != period - 1).astype(F32)

    def mixer_out(rows):
        if is_gla:
            parts = []
            for hd in range(GLA_HEADS):
                cols = slice(hd * GLA_DV, (hd + 1) * GLA_DV)
                y = _rms(o_ref[hd, rows, :], gn_ref[...]) * _silu(r_ref[rows, cols])
                parts.append(y.astype(BF16))
            y = jnp.concatenate(parts, axis=1)
        else:
            f_lat = jnp.concatenate([fl_ref[t, rows, :] for t in range(D_MODEL // LANES)], axis=1)
            y = jnp.where(i < N_CTX_BLK, fc_ref[rows, :], f_lat.astype(BF16))
        x_new = _read_rows(i, xc_ref, xl_ref, rows) + _mod_part(mod_ref, 2) * _dot(y, wo_ref[...])
        x_scr[rows, :] = x_new
        h = _rms(x_new, ng_ref[...]) * (1.0 + _mod_part(mod_ref, 4)) + _mod_part(mod_ref, 3)
        hb = h.astype(BF16)
        h_scr[rows, :] = hb
        return hb

    def ffn_tile(hb):
        ag = _dot(hb, wu_ref[...])
        a = ag[:, 0:TF]
        g = ag[:, TF:2 * TF]
        a_prev = pltpu.roll(a, 1, 0) * has_prev
        a_next = pltpu.roll(a, HALF - 1, 0) * has_next
        conv = a_prev * cw_ref[0:1, :] + a * cw_ref[1:2, :] + a_next * cw_ref[2:3, :] + cb_ref[...]
        return _dot((_silu(conv) * g).astype(BF16), wd_ref[...])

    def step(first, last):
        for s in range(TM // HALF):
            rows = slice(s * HALF, (s + 1) * HALF)
            hb = mixer_out(rows) if first else h_scr[rows, :]
            d = ffn_tile(hb)
            if not first:
                d = acc_scr[rows, :] + d
            if last:
                out = x_scr[rows, :] + _mod_part(mod_ref, 5) * d
                if is_last:
                    out_scr[rows, :] = _rms(out, fg_ref[...])
                else:
                    outc_ref[rows, :] = out
                    outl_ref[rows, :] = out
            else:
                acc_scr[rows, :] = d

    pl.when(j == 0)(lambda: step(True, n_f == 1))
    if n_f > 2:
        pl.when((j > 0) & (j < n_f - 1))(lambda: step(False, False))
    if n_f > 1:
        pl.when(j == n_f - 1)(lambda: step(False, True))

    if is_last:
        @pl.when(j == n_f - 1)
        def _():
            _write_rows(i, outc_ref, outl_ref, out_scr[...])


def _mix_ffn(is_gla, is_last, xc, xl, mix_in, w_o, mods, layer, norm_g, w_up, conv_w, conv_b,
             w_down, final_g):
    n_f = FFN_DIM // TF
    row = pl.BlockSpec((TM, D_MODEL), lambda i, j: (i, 0))
    if is_gla:
        mix_specs = [pl.BlockSpec((GLA_HEADS, TM, GLA_DV), lambda i, j: (0, i, 0)), row,
                     _layer_spec((1, GLA_DV), layer // 2)]
    else:
        mix_specs = [_ctx_rows(),
                     pl.BlockSpec((D_MODEL // LANES, TM, LANES),
                                  lambda i, j: (0, jnp.maximum(i - N_CTX_BLK, 0), 0))]
    in_specs = [_ctx_rows(), _lat_rows()] + mix_specs + [
        _layer_spec((D_MODEL, D_MODEL), layer // 2), _mod_spec(layer),
        _layer_spec((1, D_MODEL), layer),
        pl.BlockSpec((None, None, D_MODEL, 2 * TF), lambda i, j: (layer, j, 0, 0)),
        pl.BlockSpec((None, 3, TF), lambda i, j: (layer, 0, j)),
        pl.BlockSpec((None, 1, TF), lambda i, j: (layer, 0, j)),
        pl.BlockSpec((None, TF, D_MODEL), lambda i, j: (layer, j, 0)),
        _const_spec((1, D_MODEL)),
    ]
    if is_last:
        spare = 0
        out_specs = [_ctx_rows(), _lat_rows()]
    else:
        spare = TM
        n_lat_blk = N_LAT_TOK // TM
        out_specs = [
            pl.BlockSpec((TM, D_MODEL), lambda i, j: (jnp.minimum(i, N_CTX_BLK), 0)),
            pl.BlockSpec((TM, D_MODEL),
                         lambda i, j: (jnp.where(i < N_CTX_BLK, n_lat_blk, i - N_CTX_BLK), 0)),
        ]
    return pl.pallas_call(
        functools.partial(_mix_ffn_kernel, is_gla, is_last),
        grid=(N_TOK // TM, FFN_DIM // TF),
        in_specs=in_specs,
        out_specs=out_specs,
        out_shape=[jax.ShapeDtypeStruct((N_CTX_TOK + spare, D_MODEL), F32),
                   jax.ShapeDtypeStruct((N_LAT_TOK + spare, D_MODEL), F32)],
        scratch_shapes=[pltpu.VMEM((TM, D_MODEL), F32), pltpu.VMEM((TM, D_MODEL), BF16),
                        pltpu.VMEM((TM, D_MODEL), F32), pltpu.VMEM((TM, D_MODEL), F32),
                        pltpu.VMEM((D_MODEL, D_MODEL), BF16)],
        compiler_params=_params(("arbitrary", "arbitrary")),
        name="gla_out_ffn" if is_gla else "fnet_out_ffn",
    )(xc, xl, *mix_in, w_o, mods, norm_g.reshape(DEPTH, 1, D_MODEL), w_up, conv_w,
      conv_b.reshape(DEPTH, 1, FFN_DIM), w_down, final_g)


def _dft_tables(n, scale):
    idx = np.arange(n, dtype=np.int64)
    ang = (2.0 * np.pi / n) * ((idx[:, None] * idx[None, :]) % n).astype(np.float64)
    return (np.cos(ang) * scale).astype(np.float32), (np.sin(ang) * scale).astype(np.float32)


def _chunk_triangles():
    idx = np.arange(SUPER)
    same = (idx[:, None] // CHUNK) == (idx[None, :] // CHUNK)
    low = same & (idx[None, :] <= idx[:, None])
    up = same & (idx[None, :] >= idx[:, None])
    return low.astype(np.float32), up.astype(np.float32)


def kernel(x_prompt, x_sample, state_gla, c, c_ctx, norm_mix_g, norm_ffn_g, w_mod, b_mod,
           gla_w_in, gla_w_g2, gla_b_g, gla_norm_g, gla_w_o, fnet_w_in, fnet_w_o,
           ffn_w_up, ffn_conv_w, ffn_conv_b, ffn_w_down, final_norm_g):
    xc = x_prompt.reshape(N_CTX_TOK, D_MODEL)
    xl = x_sample.reshape(N_LAT_TOK, D_MODEL)
    c_cols = jnp.concatenate([c_ctx[None, :], c], axis=0)[:, :, None]
    mods = _mods(c_cols, w_mod, b_mod)

    tlow_np, tup_np = _chunk_triangles()
    tlow, tup = jnp.asarray(tlow_np).astype(BF16), jnp.asarray(tup_np).astype(BF16)
    cc, sc = _dft_tables(FNET_GW, FNET_GW ** -0.5)
    cs_chan = jnp.asarray(np.concatenate([cc, sc], axis=1)).astype(BF16)
    c_ctx_tab, s_ctx_tab = (jnp.asarray(t).astype(BF16) for t in _dft_tables(SEQ, 1.0))
    c_lat_tab, s_lat_tab = (jnp.asarray(t).astype(BF16) for t in _dft_tables(HALF_SEQ, 1.0))
    phase = (2.0 * np.pi / DEC_SEQ) * np.arange(HALF_SEQ, dtype=np.float64)
    twiddle = jnp.asarray(np.stack([np.cos(phase), np.sin(phase)], axis=1).astype(np.float32))
    final_g = final_norm_g.reshape(1, D_MODEL)

    w_up = _ffn_up_weights(ffn_w_up)
    w_down = _ffn_down_weights(ffn_w_down)
    head_norm_g = gla_norm_g.reshape(N_GLA_LAYERS, 1, GLA_DV)

    states = None
    for layer in range(DEPTH):
        j = layer // 2
        if layer % 2 == 0:
            qf, kf, qb, kb, v, r, oi, df, db = _gla_in(
                xc, xl, mods, layer, norm_mix_g, gla_w_in, gla_w_g2, gla_b_g, tlow, tup)
            o, states = _gla_scan(qf, kf, qb, kb, v, oi, df, db, state_gla, j, states)
            mix_in = (o, r, head_norm_g)
            w_o = gla_w_o
        else:
            uc, us = _fnet_in(xc, xl, mods, layer, norm_mix_g, fnet_w_in, cs_chan,
                              c_ctx_tab, s_ctx_tab)
            f_lat = _fnet_dft_lat(uc, us, c_lat_tab, s_lat_tab, twiddle)
            mix_in = (uc, f_lat)
            w_o = fnet_w_o
        xc, xl = _mix_ffn(layer % 2 == 0, layer == DEPTH - 1, xc, xl, mix_in, w_o, mods, layer,
                          norm_ffn_g, w_up, ffn_conv_w, ffn_conv_b, w_down, final_g)

    return (xc.reshape(BATCH, SEQ, D_MODEL), xl.reshape(DEC_BATCH, DEC_SEQ, D_MODEL), states)
```
